```python
import math
import jax, jax.numpy as jnp
from jax import lax
import numpy as np

D_MODEL = 1024
BATCH = 16
SEQ = 4096
DEPTH = 4

N_META = 16
GRID_W = 64
HEAD_DIM = 64
BLOCK = 128
A_HEADS = 4
B_HEADS = 8
NA_ROWS_MAX = 8
NA_COLS = 16
C_HEADS = 8
C_KV_HEADS = 2
WINDOW = 128
T5_BUCKETS = 32
T5_MAX_DIST = 128
D_FF = 2816
N_BRANCH = 3
BRANCH_W = 512
A_COLS = 3 * A_HEADS * 2 * HEAD_DIM
B_COLS = 3 * B_HEADS * HEAD_DIM
C_COLS = (C_HEADS + 2 * C_KV_HEADS) * HEAD_DIM
IN_COLS = A_COLS + B_COLS + C_COLS
EPS = 1e-6
NEG = -1e30

kernel_name = "hybrid_gated_diff_natten_swa_encoder"


def rms_norm(x, g):
    xf = x.astype(jnp.float32)
    y = xf * lax.rsqrt(jnp.mean(xf * xf, axis=-1, keepdims=True) + EPS)
    return (y * g.astype(jnp.float32)).astype(x.dtype)


def swiglu(x, w_in, w_out):
    g, u = jnp.split(x @ w_in, 2, axis=-1)
    return (jax.nn.silu(g) * u) @ w_out


def t5_bucket(rel):
    nb = T5_BUCKETS // 2
    max_exact = nb // 2
    ret = jnp.where(rel > 0, nb, 0)
    n = jnp.abs(rel)
    nf = jnp.maximum(n, 1).astype(jnp.float32)
    large = max_exact + (jnp.log(nf / max_exact) / math.log(T5_MAX_DIST / max_exact)
                         * (nb - max_exact)).astype(jnp.int32)
    large = jnp.minimum(large, nb - 1)
    return ret + jnp.where(n < max_exact, n, large)


def t5_bias(table, rel):
    return jnp.moveaxis(table[t5_bucket(rel)], -1, 0).astype(jnp.float32)


def sink_softmax(s, sink):
    m = jnp.maximum(jnp.max(s, axis=-1, keepdims=True), sink)
    e = jnp.exp(s - m)
    return e / (jnp.sum(e, axis=-1, keepdims=True) + jnp.exp(sink - m))


def diff_attention(q, k, v, lam, lam_init, subln_g, table):
    bsz, L, H, _, dh = q.shape
    S = L - N_META
    nblk = S // BLOCK
    scale = dh ** -0.5
    kpos = jnp.arange(L)

    def attend(qb, qpos):
        s = jnp.einsum("bqhmd,bkhmd->bhmqk", qb, k).astype(jnp.float32) * scale
        s = s + t5_bias(table, kpos[None, :] - qpos[:, None])[None, :, None]
        p = jax.nn.softmax(s, axis=-1)
        a = p[:, :, 0] - lam * p[:, :, 1]
        return jnp.einsum("bhqk,bkhe->bqhe", a.astype(v.dtype), v)

    o_meta = attend(q[:, :N_META], jnp.arange(N_META))
    q_blocks = jnp.swapaxes(q[:, N_META:].reshape(bsz, nblk, BLOCK, H, 2, dh), 0, 1)
    pos_blocks = N_META + jnp.arange(S).reshape(nblk, BLOCK)
    o_real = lax.map(lambda a: attend(a[0], a[1]), (q_blocks, pos_blocks))
    o_real = jnp.swapaxes(o_real, 0, 1).reshape(bsz, S, H, 2 * dh)
    o = jnp.concatenate([o_meta, o_real], axis=1)
    o = rms_norm(o, subln_g) * (1.0 - lam_init)
    return o.reshape(bsz, L, H * 2 * dh)


def neighborhood_attention(q, k, v, rpb):
    bsz, L, H, dh = q.shape
    S = L - N_META
    rows = S // GRID_W
    wr = min(NA_ROWS_MAX, rows)
    scale = dh ** -0.5
    qm, km, vm = q[:, :N_META], k[:, :N_META], v[:, :N_META]
    qg = q[:, N_META:].reshape(bsz, rows, GRID_W, H, dh)
    kg = k[:, N_META:].reshape(bsz, rows, GRID_W, H, dh)
    vg = v[:, N_META:].reshape(bsz, rows, GRID_W, H, dh)

    cols = jnp.arange(GRID_W)
    cstart = jnp.clip(cols - NA_COLS // 2, 0, GRID_W - NA_COLS)
    col_ok = (cols[None, :] >= cstart[:, None]) & (cols[None, :] < cstart[:, None] + NA_COLS)
    col_idx = jnp.clip(cols[None, :] - cols[:, None] + NA_COLS - 1, 0, 2 * NA_COLS - 2)
    rb_cols = rpb[:, :, col_idx]

    def row(args):
        q_row, r = args
        rs = jnp.clip(r - wr // 2, 0, rows - wr)
        k_nb = lax.dynamic_slice_in_dim(kg, rs, wr, axis=1)
        v_nb = lax.dynamic_slice_in_dim(vg, rs, wr, axis=1)
        s = jnp.einsum("bchd,bxyhd->bhcxy", q_row, k_nb).astype(jnp.float32) * scale
        row_off = rs + jnp.arange(wr) - r + NA_ROWS_MAX - 1
        bias = jnp.transpose(rb_cols[:, row_off], (0, 2, 1, 3)).astype(jnp.float32)
        s = jnp.where(col_ok[:, None, :], s + bias[None], NEG).reshape(bsz, H, GRID_W, wr * GRID_W)
        sm = jnp.einsum("bchd,bmhd->bhcm", q_row, km).astype(jnp.float32) * scale
        p = jax.nn.softmax(jnp.concatenate([sm, s], axis=-1), axis=-1).astype(v.dtype)
        p_nb = p[..., N_META:].reshape(bsz, H, GRID_W, wr, GRID_W)
        return (jnp.einsum("bhcm,bmhd->bchd", p[..., :N_META], vm)
                + jnp.einsum("bhcxy,bxyhd->bchd", p_nb, v_nb))

    o_real = lax.map(row, (jnp.swapaxes(qg, 0, 1), jnp.arange(rows)))
    o_real = jnp.swapaxes(o_real, 0, 1).reshape(bsz, S, H, dh)

    k_org = kg[:, :wr, :NA_COLS].reshape(bsz, wr * NA_COLS, H, dh)
    v_org = vg[:, :wr, :NA_COLS].reshape(bsz, wr * NA_COLS, H, dh)
    k_mq = jnp.concatenate([km, k_org], axis=1)
    v_mq = jnp.concatenate([vm, v_org], axis=1)
    s_m = jnp.einsum("bqhd,bkhd->bhqk", qm, k_mq).astype(jnp.float32) * scale
    p_m = jax.nn.softmax(s_m, axis=-1).astype(v.dtype)
    o_meta = jnp.einsum("bhqk,bkhd->bqhd", p_m, v_mq)
    return jnp.concatenate([o_meta, o_real], axis=1).reshape(bsz, L, H * dh)


def window_gqa(q, k, v, sink, table):
    bsz, L, HQ, dh = q.shape
    KV = k.shape[2]
    G = HQ // KV
    S = L - N_META
    nblk = S // BLOCK
    scale = dh ** -0.5
    q = q.reshape(bsz, L, KV, G, dh)
    qm, km, vm = q[:, :N_META], k[:, :N_META], v[:, :N_META]
    sink_b = sink.astype(jnp.float32).reshape(KV, G, 1, 1)
    mpos = jnp.arange(N_META)

    pad = ((0, 0), (BLOCK, BLOCK), (0, 0), (0, 0))
    kp = jnp.pad(k[:, N_META:], pad)
    vp = jnp.pad(v[:, N_META:], pad)
    qi = jnp.arange(BLOCK)
    ki = jnp.arange(3 * BLOCK) - BLOCK
    rel = ki[None, :] - qi[:, None]
    band_bias = t5_bias(table, rel).reshape(KV, G, BLOCK, 3 * BLOCK)

    def block(args):
        qb, j = args
        kb = lax.dynamic_slice_in_dim(kp, j * BLOCK, 3 * BLOCK, axis=1)
        vb = lax.dynamic_slice_in_dim(vp, j * BLOCK, 3 * BLOCK, axis=1)
        kabs = j * BLOCK + ki
        valid = (jnp.abs(rel) <= WINDOW) & ((kabs >= 0) & (kabs < S))[None, :]
        s_b = jnp.einsum("bqkgd,bjkd->bkgqj", qb, kb).astype(jnp.float32) * scale + band_bias
        s_b = jnp.where(valid, s_b, NEG)
        qpos = N_META + j * BLOCK + qi
        s_m = (jnp.einsum("bqkgd,bmkd->bkgqm", qb, km).astype(jnp.float32) * scale
               + t5_bias(table, mpos[None, :] - qpos[:, None]).reshape(KV, G, BLOCK, N_META))
        p = sink_softmax(jnp.concatenate([s_m, s_b], axis=-1), sink_b).astype(v.dtype)
        return (jnp.einsum("bkgqm,bmkd->bqkgd", p[..., :N_META], vm)
                + jnp.einsum("bkgqj,bjkd->bqkgd", p[..., N_META:], vb))

    q_blocks = jnp.swapaxes(q[:, N_META:].reshape(bsz, nblk, BLOCK, KV, G, dh), 0, 1)
    o_real = lax.map(block, (q_blocks, jnp.arange(nblk)))
    o_real = jnp.swapaxes(o_real, 0, 1).reshape(bsz, S, KV, G, dh)

    k0 = k[:, N_META:N_META + BLOCK]
    v0 = v[:, N_META:N_META + BLOCK]
    rel0 = (N_META + jnp.arange(BLOCK))[None, :] - mpos[:, None]
    s_mm = (jnp.einsum("bqkgd,bmkd->bkgqm", qm, km).astype(jnp.float32) * scale
            + t5_bias(table, mpos[None, :] - mpos[:, None]).reshape(KV, G, N_META, N_META))
    s_m0 = (jnp.einsum("bqkgd,bjkd->bkgqj", qm, k0).astype(jnp.float32) * scale
            + t5_bias(table, rel0).reshape(KV, G, N_META, BLOCK))
    s_m0 = jnp.where(rel0 <= WINDOW, s_m0, NEG)
    p_m = sink_softmax(jnp.concatenate([s_mm, s_m0], axis=-1), sink_b).astype(v.dtype)
    o_meta = (jnp.einsum("bkgqm,bmkd->bqkgd", p_m[..., :N_META], vm)
              + jnp.einsum("bkgqj,bjkd->bqkgd", p_m[..., N_META:], v0))
    return jnp.concatenate([o_meta, o_real], axis=1).reshape(bsz, L, HQ * dh)


def token_mixer(xn, w_in, lam_q1, lam_k1, lam_q2, lam_k2, subln_g, rpb, sink, t5_table,
                w_branch, w_gate, w_out, lam_init):
    bsz, L, _ = xn.shape
    proj = xn @ w_in
    pa, pb, pc = jnp.split(proj, [A_COLS, A_COLS + B_COLS], axis=-1)

    qa, ka, va = jnp.split(pa, 3, axis=-1)
    qa = qa.reshape(bsz, L, A_HEADS, 2, HEAD_DIM)
    ka = ka.reshape(bsz, L, A_HEADS, 2, HEAD_DIM)
    va = va.reshape(bsz, L, A_HEADS, 2 * HEAD_DIM)
    f32 = jnp.float32
    lam = (jnp.exp(jnp.sum(lam_q1.astype(f32) * lam_k1.astype(f32)))
           - jnp.exp(jnp.sum(lam_q2.astype(f32) * lam_k2.astype(f32))) + lam_init)
    ya = diff_attention(qa, ka, va, lam, lam_init, subln_g, t5_table[:, :A_HEADS])

    qb, kb, vb = [t.reshape(bsz, L, B_HEADS, HEAD_DIM) for t in jnp.split(pb, 3, axis=-1)]
    yb = neighborhood_attention(qb, kb, vb, rpb)

    qc, kc, vc = jnp.split(pc, [C_HEADS * HEAD_DIM, (C_HEADS + C_KV_HEADS) * HEAD_DIM], axis=-1)
    qc = qc.reshape(bsz, L, C_HEADS, HEAD_DIM)
    kc = kc.reshape(bsz, L, C_KV_HEADS, HEAD_DIM)
    vc = vc.reshape(bsz, L, C_KV_HEADS, HEAD_DIM)
    yc = window_gqa(qc, kc, vc, sink, t5_table[:, A_HEADS:])

    merged = (jax.nn.sigmoid(xn @ w_gate[0]) * (ya @ w_branch[0])
              + jax.nn.sigmoid(xn @ w_gate[1]) * (yb @ w_branch[1])
              + jax.nn.sigmoid(xn @ w_gate[2]) * (yc @ w_branch[2]))
    return merged @ w_out


def setup_inputs(seed: int = 0) -> dict:
    key = jax.random.key(seed)
    ks = jax.random.split(key, 24)
    f32 = jnp.float32
    D = D_MODEL

    def nrm(k, shape, scale):
        return jax.random.normal(k, shape, f32) * scale

    def gain(k, shape):
        return 1.0 + 0.05 * jax.random.normal(k, shape, f32)

    return {
        "x": nrm(ks[0], (BATCH, SEQ, D), 1.0),
        "meta_tokens": nrm(ks[1], (N_META, D), 1.0),
        "t5_table": nrm(ks[2], (T5_BUCKETS, A_HEADS + C_HEADS), 0.5),
        "norm_ffn1": gain(ks[3], (DEPTH, D)),
        "w_ffn1_in": nrm(ks[4], (DEPTH, D, 2 * D_FF), D ** -0.5),
        "w_ffn1_out": nrm(ks[5], (DEPTH, D_FF, D), D_FF ** -0.5),
        "norm_mix": gain(ks[6], (DEPTH, D)),
        "w_in": nrm(ks[7], (DEPTH, D, IN_COLS), D ** -0.5),
        "lambda_q1": nrm(ks[8], (DEPTH, HEAD_DIM), 0.1),
        "lambda_k1": nrm(ks[9], (DEPTH, HEAD_DIM), 0.1),
        "lambda_q2": nrm(ks[10], (DEPTH, HEAD_DIM), 0.1),
        "lambda_k2": nrm(ks[11], (DEPTH, HEAD_DIM), 0.1),
        "subln_gain": gain(ks[12], (DEPTH, 2 * HEAD_DIM)),
        "natten_rpb": nrm(ks[13], (DEPTH, B_HEADS, 2 * NA_ROWS_MAX - 1, 2 * NA_COLS - 1), 0.5),
        "sink_logits": nrm(ks[14], (DEPTH, C_HEADS), 0.5),
        "w_branch": nrm(ks[15], (DEPTH, N_BRANCH, BRANCH_W, D), BRANCH_W ** -0.5),
        "w_gate": nrm(ks[16], (DEPTH, N_BRANCH, D, D), D ** -0.5),
        "w_out": nrm(ks[17], (DEPTH, D, D), D ** -0.5),
        "norm_ffn2": gain(ks[18], (DEPTH, D)),
        "w_ffn2_in": nrm(ks[19], (DEPTH, D, 2 * D_FF), D ** -0.5),
        "w_ffn2_out": nrm(ks[20], (DEPTH, D_FF, D), D_FF ** -0.5),
        "final_norm": gain(ks[21], (D,)),
    }


def reference(x, meta_tokens, t5_table, norm_ffn1, w_ffn1_in, w_ffn1_out, norm_mix, w_in,
              lambda_q1, lambda_k1, lambda_q2, lambda_k2, subln_gain, natten_rpb, sink_logits,
              w_branch, w_gate, w_out, norm_ffn2, w_ffn2_in, w_ffn2_out, final_norm):
    bsz = x.shape[0]
    meta = jnp.broadcast_to(meta_tokens[None].astype(x.dtype), (bsz, N_META, D_MODEL))
    h = jnp.concatenate([meta, x], axis=1)
    for l in range(DEPTH):
        lam_init = 0.8 - 0.6 * math.exp(-0.3 * l)
        h = h + 0.5 * swiglu(rms_norm(h, norm_ffn1[l]), w_ffn1_in[l], w_ffn1_out[l])
        h = h + token_mixer(rms_norm(h, norm_mix[l]), w_in[l], lambda_q1[l], lambda_k1[l],
                            lambda_q2[l], lambda_k2[l], subln_gain[l], natten_rpb[l], sink_logits[l],
                            t5_table, w_branch[l], w_gate[l], w_out[l], lam_init)
        h = h + 0.5 * swiglu(rms_norm(h, norm_ffn2[l]), w_ffn2_in[l], w_ffn2_out[l])
    return rms_norm(h, final_norm)[:, N_META:]
```

```python
import functools
import math

import numpy as np
import jax
import jax.numpy as jnp
from jax import lax
from jax.experimental import pallas as pl
from jax.experimental.pallas import tpu as pltpu

F32 = jnp.float32
BF16 = jnp.bfloat16

D_MODEL = 1024
SEQ = 4096
N_META = 16
GRID_W = 64
HEAD_DIM = 64
LANES = 128
A_HEADS = 4
B_HEADS = 8
C_HEADS = 8
C_KV_HEADS = 2
NA_ROWS = 8
NA_COLS = 16
WINDOW = 128
T5_BUCKETS = 32
D_FF = 2816
BRANCH_W = 512
IN_COLS = 3840
EPS = 1e-6
NEG = -1e30
SCALE = HEAD_DIM ** -0.5

QA_BLK, KA_BLK, VA_BLK, QB_BLK, KB_BLK, VB_BLK, QC_BLK = 0, 1, 2, 3, 4, 5, 6
KC_BLK128, VC_BLK128 = 28, 29
C_PERM = (0, 4, 1, 5, 2, 6, 3, 7)

TM_REAL = 512
FF_CHUNK = 256
PROJ_CHUNK = 768
TQ = 256
TK = 256
NAT_R = 4
NAT_W = 12
GQA_KW = 512
VMEM_LIMIT = 56 * 1024 * 1024


def _cparams(n_axes):
    return pltpu.CompilerParams(dimension_semantics=("parallel",) * n_axes,
                                vmem_limit_bytes=VMEM_LIMIT)


def _const_spec(shape, n_grid):
    zeros = (0,) * len(shape)
    if n_grid == 1:
        return pl.BlockSpec(shape, lambda i: zeros)
    return pl.BlockSpec(shape, lambda i, j: zeros)


def _rms(x, g):
    return x * lax.rsqrt(jnp.mean(x * x, axis=-1, keepdims=True) + EPS) * g


def _dot(a, b):
    return jnp.dot(a, b, preferred_element_type=F32)


def _dot_t(a, b):
    return lax.dot_general(a, b, (((1,), (1,)), ((), ())), preferred_element_type=F32)


def _ffn_kernel(h_ref, g_ref, win_ref, wout_ref, *rest, final):
    if final:
        fg_ref, o_ref, act_ref = rest
    else:
        o_ref, act_ref = rest
    x = h_ref[...]
    xn = _rms(x, g_ref[...]).astype(BF16)
    for c in range(D_FF // FF_CHUNK):
        lo = c * FF_CHUNK
        gg = _dot(xn, win_ref[:, lo:lo + FF_CHUNK])
        uu = _dot(xn, win_ref[:, D_FF + lo:D_FF + lo + FF_CHUNK])
        act_ref[:, lo:lo + FF_CHUNK] = (gg * jax.nn.sigmoid(gg) * uu).astype(BF16)
    hn = x + 0.5 * _dot(act_ref[...], wout_ref[...])
    if final:
        hn = _rms(hn, fg_ref[...])
    o_ref[...] = hn


def _ffn(h, gain, w_in, w_out, final_gain=None):
    rows = h.shape[0]
    tm = min(TM_REAL, rows)
    final = final_gain is not None
    in_specs = [
        pl.BlockSpec((tm, D_MODEL), lambda i: (i, 0)),
        _const_spec((1, D_MODEL), 1),
        _const_spec((D_MODEL, 2 * D_FF), 1),
        _const_spec((D_FF, D_MODEL), 1),
    ]
    args = [h, gain.reshape(1, D_MODEL), w_in, w_out]
    if final:
        in_specs.append(_const_spec((1, D_MODEL), 1))
        args.append(final_gain.reshape(1, D_MODEL))
    return pl.pallas_call(
        functools.partial(_ffn_kernel, final=final),
        grid=(rows // tm,),
        in_specs=in_specs,
        out_specs=pl.BlockSpec((tm, D_MODEL), lambda i: (i, 0)),
        out_shape=jax.ShapeDtypeStruct((rows, D_MODEL), F32),
        scratch_shapes=[pltpu.VMEM((tm, D_FF), BF16)],
        compiler_params=_cparams(1),
        name="ffn",
    )(*args)


def _inproj_kernel(h_ref, g_ref, w_ref, o_ref):
    xn = _rms(h_ref[...], g_ref[...]).astype(BF16)
    for c in range(IN_COLS // PROJ_CHUNK):
        lo = c * PROJ_CHUNK
        o_ref[:, lo:lo + PROJ_CHUNK] = _dot(xn, w_ref[:, lo:lo + PROJ_CHUNK]).astype(BF16)


def _inproj(h, gain, w_in):
    rows = h.shape[0]
    tm = min(TM_REAL, rows)
    return pl.pallas_call(
        _inproj_kernel,
        grid=(rows // tm,),
        in_specs=[
            pl.BlockSpec((tm, D_MODEL), lambda i: (i, 0)),
            _const_spec((1, D_MODEL), 1),
            _const_spec((D_MODEL, IN_COLS), 1),
        ],
        out_specs=pl.BlockSpec((tm, IN_COLS), lambda i: (i, 0)),
        out_shape=jax.ShapeDtypeStruct((rows, IN_COLS), BF16),
        compiler_params=_cparams(1),
        name="inproj",
    )(h, gain.reshape(1, D_MODEL), w_in)


def _merge_kernel(h_ref, g_ref, ya_ref, yb_ref, yc_ref, wg_ref, wb_ref, wo_ref, o_ref):
    x = h_ref[...]
    xn = _rms(x, g_ref[...]).astype(BF16)
    merged = None
    for i, y_ref in enumerate((ya_ref, yb_ref, yc_ref)):
        term = jax.nn.sigmoid(_dot(xn, wg_ref[i])) * _dot(y_ref[...], wb_ref[i])
        merged = term if merged is None else merged + term
    o_ref[...] = x + _dot(merged.astype(BF16), wo_ref[...])


def _merge(h, gain, ya, yb, yc, w_gate, w_branch, w_out):
    rows = h.shape[0]
    tm = min(TM_REAL, rows)
    row_spec = lambda w: pl.BlockSpec((tm, w), lambda i: (i, 0))
    return pl.pallas_call(
        _merge_kernel,
        grid=(rows // tm,),
        in_specs=[
            row_spec(D_MODEL),
            _const_spec((1, D_MODEL), 1),
            row_spec(BRANCH_W), row_spec(BRANCH_W), row_spec(BRANCH_W),
            _const_spec((3, D_MODEL, D_MODEL), 1),
            _const_spec((3, BRANCH_W, D_MODEL), 1),
            _const_spec((D_MODEL, D_MODEL), 1),
        ],
        out_specs=row_spec(D_MODEL),
        out_shape=jax.ShapeDtypeStruct((rows, D_MODEL), F32),
        compiler_params=_cparams(1),
        name="merge",
    )(h, gain.reshape(1, D_MODEL), ya, yb, yc, w_gate, w_branch, w_out)


def _lane_lo(rows):
    return lax.broadcasted_iota(jnp.int32, (rows, LANES), 1) < HEAD_DIM


def _split_halves(q):
    lo = _lane_lo(q.shape[0])
    zero = jnp.zeros_like(q)
    return jnp.concatenate([jnp.where(lo, q, zero), jnp.where(lo, zero, q)], axis=0)


def _pad_rows(x, rows):
    return jnp.concatenate([x, jnp.zeros((rows - x.shape[0], x.shape[1]), x.dtype)], axis=0)


def _twice(b):
    return jnp.concatenate([b, b], axis=0)


def _lam(lq1_ref, lk1_ref, lq2_ref, lk2_ref, lam_init):
    s1 = jnp.sum(lq1_ref[...] * lk1_ref[...], axis=-1, keepdims=True)
    s2 = jnp.sum(lq2_ref[...] * lk2_ref[...], axis=-1, keepdims=True)
    return jnp.exp(s1) - jnp.exp(s2) + lam_init


def _diff_finish(o, lam, g, lam_init):
    m = o.shape[0] // 2
    d = o[:m] - lam * o[m:]
    return _rms(d, g) * (1.0 - lam_init)


def _diff_real_kernel(lq1_ref, lk1_ref, lq2_ref, lk2_ref, g_ref, q_ref, k_ref, v_ref, km_ref, vm_ref,
                      bias_ref, biasm_ref, o_ref, *, lam_init):
    i = pl.program_id(1)
    lam = _lam(lq1_ref, lk1_ref, lq2_ref, lk2_ref, lam_init)
    g = g_ref[...]
    nk = k_ref.shape[0] // TK
    for h in range(A_HEADS):
        hs = slice(h * LANES, (h + 1) * LANES)
        q12 = _split_halves(q_ref[:, hs])
        km = _pad_rows(km_ref[:, hs], LANES)
        vm = _pad_rows(vm_ref[:, hs], LANES)
        s = _dot_t(q12, km) + _twice(biasm_ref[h])
        m0 = jnp.max(s, axis=-1, keepdims=True)
        p = jnp.exp(s - m0)
        l0 = jnp.sum(p, axis=-1, keepdims=True)
        acc0 = _dot(p.astype(BF16), vm)

        def body(j, carry, hs=hs, h=h, q12=q12):
            m, l, acc = carry
            start = pl.multiple_of(j * TK, TK)
            kt = k_ref[pl.ds(start, TK), hs]
            vt = v_ref[pl.ds(start, TK), hs]
            d = jnp.clip(j - i, -2, 2) + 2
            s = _dot_t(q12, kt) + _twice(bias_ref[h, d])
            m_new = jnp.maximum(m, jnp.max(s, axis=-1, keepdims=True))
            alpha = jnp.exp(m - m_new)
            p = jnp.exp(s - m_new)
            l = alpha * l + jnp.sum(p, axis=-1, keepdims=True)
            acc = alpha * acc + _dot(p.astype(BF16), vt)
            return m_new, l, acc

        _, l, acc = lax.fori_loop(0, nk, body, (m0, l0, acc0))
        o_ref[:, hs] = _diff_finish(acc / l, lam, g, lam_init).astype(BF16)


def _diff_real(proj_r, proj_m, lvecs, g_sub, bias5, biasm, lam_init, nb):
    nq = SEQ // TQ
    lspec = _const_spec((1, HEAD_DIM), 2)
    return pl.pallas_call(
        functools.partial(_diff_real_kernel, lam_init=lam_init),
        grid=(nb, nq),
        in_specs=[
            lspec, lspec, lspec, lspec,
            _const_spec((1, LANES), 2),
            pl.BlockSpec((TQ, 512), lambda b, i: (b * nq + i, QA_BLK)),
            pl.BlockSpec((SEQ, 512), lambda b, i: (b, KA_BLK)),
            pl.BlockSpec((SEQ, 512), lambda b, i: (b, VA_BLK)),
            pl.BlockSpec((N_META, 512), lambda b, i: (b, KA_BLK)),
            pl.BlockSpec((N_META, 512), lambda b, i: (b, VA_BLK)),
            _const_spec((A_HEADS, 5, TQ, TK), 2),
            pl.BlockSpec((A_HEADS, TQ, LANES), lambda b, i: (0, i, 0)),
        ],
        out_specs=pl.BlockSpec((TQ, 512), lambda b, i: (b * nq + i, 0)),
        out_shape=jax.ShapeDtypeStruct((nb * SEQ, BRANCH_W), BF16),
        compiler_params=_cparams(2),
        name="diff_real",
    )(*lvecs, g_sub, proj_r, proj_r, proj_r, proj_m, proj_m, bias5, biasm)


def _nat_real_kernel(q_ref, k_ref, v_ref, km_ref, vm_ref, bias_ref, o_ref):
    blk = pl.program_id(1)
    n_rows = k_ref.shape[0] // GRID_W
    w0 = jnp.clip(NAT_R * blk - NA_ROWS // 2, 0, n_rows - NAT_W)
    start = pl.multiple_of(w0 * GRID_W, GRID_W)
    nkeys = NAT_W * GRID_W
    pad_bias = jnp.where(lax.broadcasted_iota(jnp.int32, (1, LANES), 1) < N_META, 0.0, NEG)
    lo = _lane_lo(TQ)
    for pr in range(B_HEADS // 2):
        hs = slice(pr * LANES, (pr + 1) * LANES)
        q2 = _split_halves(q_ref[:, hs])
        kw = k_ref[pl.ds(start, nkeys), hs]
        vw = v_ref[pl.ds(start, nkeys), hs]
        km = _pad_rows(km_ref[:, hs], LANES)
        vm = _pad_rows(vm_ref[:, hs], LANES)
        s = _dot_t(q2, kw) + jnp.concatenate([bias_ref[0, 2 * pr], bias_ref[0, 2 * pr + 1]], axis=0)
        sm = _dot_t(q2, km) + pad_bias
        m = jnp.maximum(jnp.max(s, axis=-1, keepdims=True), jnp.max(sm, axis=-1, keepdims=True))
        e = jnp.exp(s - m)
        em = jnp.exp(sm - m)
        l = jnp.sum(e, axis=-1, keepdims=True) + jnp.sum(em, axis=-1, keepdims=True)
        o = (_dot(e.astype(BF16), vw) + _dot(em.astype(BF16), vm)) / l
        o_ref[:, hs] = jnp.where(lo, o[:TQ], o[TQ:]).astype(BF16)


def _nat_real(proj_r, proj_m, bias, nb):
    nq = SEQ // TQ
    nkeys = NAT_W * GRID_W

    def variant(b, i):
        return (jnp.where(i == 0, 0, jnp.where(i == nq - 1, 2, 1)), 0, 0, 0)

    return pl.pallas_call(
        _nat_real_kernel,
        grid=(nb, nq),
        in_specs=[
            pl.BlockSpec((TQ, 512), lambda b, i: (b * nq + i, QB_BLK)),
            pl.BlockSpec((SEQ, 512), lambda b, i: (b, KB_BLK)),
            pl.BlockSpec((SEQ, 512), lambda b, i: (b, VB_BLK)),
            pl.BlockSpec((N_META, 512), lambda b, i: (b, KB_BLK)),
            pl.BlockSpec((N_META, 512), lambda b, i: (b, VB_BLK)),
            pl.BlockSpec((1, B_HEADS, TQ, nkeys), variant),
        ],
        out_specs=pl.BlockSpec((TQ, 512), lambda b, i: (b * nq + i, 0)),
        out_shape=jax.ShapeDtypeStruct((nb * SEQ, BRANCH_W), BF16),
        compiler_params=_cparams(2),
        name="nat_real",
    )(proj_r, proj_r, proj_r, proj_m, proj_m, bias)


def _gqa_real_kernel(sink_ref, q_ref, k_ref, v_ref, km_ref, vm_ref, band_ref, mb_ref, o_ref):
    t = pl.program_id(1)
    start = pl.multiple_of(jnp.clip(TQ * t - WINDOW, 0, k_ref.shape[0] - GQA_KW), WINDOW)
    kw = k_ref[pl.ds(start, GQA_KW), :]
    vw = v_ref[pl.ds(start, GQA_KW), :]
    km = _pad_rows(km_ref[...], LANES)
    vm = _pad_rows(vm_ref[...], LANES)
    lo = _lane_lo(TQ)
    npair = C_HEADS // 2
    for pr in range(npair):
        hs = slice(pr * LANES, (pr + 1) * LANES)
        q2 = _split_halves(q_ref[:, hs])
        s = _dot_t(q2, kw) + jnp.concatenate([band_ref[0, pr], band_ref[0, npair + pr]], axis=0)
        sm = _dot_t(q2, km) + jnp.concatenate([mb_ref[pr], mb_ref[npair + pr]], axis=0)
        sink = jnp.concatenate([jnp.full((TQ, 1), sink_ref[pr], F32),
                                jnp.full((TQ, 1), sink_ref[npair + pr], F32)], axis=0)
        m = jnp.maximum(jnp.maximum(jnp.max(s, axis=-1, keepdims=True), jnp.max(sm, axis=-1, keepdims=True)), sink)
        e = jnp.exp(s - m)
        em = jnp.exp(sm - m)
        l = jnp.sum(e, axis=-1, keepdims=True) + jnp.sum(em, axis=-1, keepdims=True) + jnp.exp(sink - m)
        o = (_dot(e.astype(BF16), vw) + _dot(em.astype(BF16), vm)) / l
        o_ref[:, hs] = jnp.where(lo, o[:TQ], o[TQ:]).astype(BF16)


def _gqa_real(proj_r, proj_m, sink, band, mbias, nb):
    nq = SEQ // TQ

    def variant(b, i):
        return (jnp.where(i == 0, 0, jnp.where(i == nq - 1, 2, 1)), 0, 0, 0)

    return pl.pallas_call(
        _gqa_real_kernel,
        grid=(nb, nq),
        in_specs=[
            pl.BlockSpec(memory_space=pltpu.SMEM),
            pl.BlockSpec((TQ, 512), lambda b, i: (b * nq + i, QC_BLK)),
            pl.BlockSpec((SEQ, LANES), lambda b, i: (b, KC_BLK128)),
            pl.BlockSpec((SEQ, LANES), lambda b, i: (b, VC_BLK128)),
            pl.BlockSpec((N_META, LANES), lambda b, i: (b, KC_BLK128)),
            pl.BlockSpec((N_META, LANES), lambda b, i: (b, VC_BLK128)),
            pl.BlockSpec((1, C_HEADS, TQ, GQA_KW), variant),
            pl.BlockSpec((C_HEADS, TQ, LANES), lambda b, i: (0, i, 0)),
        ],
        out_specs=pl.BlockSpec((TQ, 512), lambda b, i: (b * nq + i, 0)),
        out_shape=jax.ShapeDtypeStruct((nb * SEQ, BRANCH_W), BF16),
        compiler_params=_cparams(2),
        name="gqa_real",
    )(sink, proj_r, proj_r, proj_r, proj_m, proj_m, band, mbias)


def _meta_kernel(sink_ref, lq1_ref, lk1_ref, lq2_ref, lk2_ref, g_ref, pm_ref, ka_ref, va_ref, kb_ref, vb_ref,
                 kc_ref, vc_ref, abr_ref, abm_ref, cbr_ref, cbm_ref, oa_ref, ob_ref, oc_ref, *, lam_init):
    lam = _lam(lq1_ref, lk1_ref, lq2_ref, lk2_ref, lam_init)
    g = g_ref[...]
    lo = _lane_lo(N_META)
    lane = lax.broadcasted_iota(jnp.int32, (1, LANES), 1)
    pad_bias = jnp.where(lane < N_META, 0.0, NEG)

    def col(blk512, sub):
        base = blk512 * 512 + sub * LANES
        return slice(base, base + LANES)

    for h in range(A_HEADS):
        hs = slice(h * LANES, (h + 1) * LANES)
        q12 = _split_halves(pm_ref[:, col(QA_BLK, h)])
        km = _pad_rows(pm_ref[:, col(KA_BLK, h)], LANES)
        vm = _pad_rows(pm_ref[:, col(VA_BLK, h)], LANES)
        s = _dot_t(q12, ka_ref[:, hs]) + _twice(abr_ref[h])
        sm = _dot_t(q12, km) + _twice(abm_ref[h])
        m = jnp.maximum(jnp.max(s, axis=-1, keepdims=True), jnp.max(sm, axis=-1, keepdims=True))
        e = jnp.exp(s - m)
        em = jnp.exp(sm - m)
        l = jnp.sum(e, axis=-1, keepdims=True) + jnp.sum(em, axis=-1, keepdims=True)
        o = (_dot(e.astype(BF16), va_ref[:, hs]) + _dot(em.astype(BF16), vm)) / l
        oa_ref[:, hs] = _diff_finish(o, lam, g, lam_init).astype(BF16)

    org = jnp.bitwise_and(lax.broadcasted_iota(jnp.int32, (1, NA_ROWS * GRID_W), 1), GRID_W - 1) < NA_COLS
    org_bias = jnp.where(org, 0.0, NEG)
    for pr in range(B_HEADS // 2):
        hs = slice(pr * LANES, (pr + 1) * LANES)
        q2 = _split_halves(pm_ref[:, col(QB_BLK, pr)])
        km = _pad_rows(pm_ref[:, col(KB_BLK, pr)], LANES)
        vm = _pad_rows(pm_ref[:, col(VB_BLK, pr)], LANES)
        s = _dot_t(q2, kb_ref[:, hs]) + org_bias
        sm = _dot_t(q2, km) + pad_bias
        m = jnp.maximum(jnp.max(s, axis=-1, keepdims=True), jnp.max(sm, axis=-1, keepdims=True))
        e = jnp.exp(s - m)
        em = jnp.exp(sm - m)
        l = jnp.sum(e, axis=-1, keepdims=True) + jnp.sum(em, axis=-1, keepdims=True)
        o = (_dot(e.astype(BF16), vb_ref[:, hs]) + _dot(em.astype(BF16), vm)) / l
        ob_ref[:, hs] = jnp.where(lo, o[:N_META], o[N_META:]).astype(BF16)

    kc_base = KC_BLK128 * LANES
    vc_base = VC_BLK128 * LANES
    km = _pad_rows(pm_ref[:, kc_base:kc_base + LANES], LANES)
    vm = _pad_rows(pm_ref[:, vc_base:vc_base + LANES], LANES)
    npair = C_HEADS // 2
    for pr in range(npair):
        hs = slice(pr * LANES, (pr + 1) * LANES)
        q2 = _split_halves(pm_ref[:, col(QC_BLK, pr)])
        s = _dot_t(q2, kc_ref[...]) + jnp.concatenate([cbr_ref[pr], cbr_ref[npair + pr]], axis=0)
        sm = _dot_t(q2, km) + jnp.concatenate([cbm_ref[pr], cbm_ref[npair + pr]], axis=0)
        sink = jnp.concatenate([jnp.full((N_META, 1), sink_ref[pr], F32),
                                jnp.full((N_META, 1), sink_ref[npair + pr], F32)], axis=0)
        m = jnp.maximum(jnp.maximum(jnp.max(s, axis=-1, keepdims=True), jnp.max(sm, axis=-1, keepdims=True)), sink)
        e = jnp.exp(s - m)
        em = jnp.exp(sm - m)
        l = jnp.sum(e, axis=-1, keepdims=True) + jnp.sum(em, axis=-1, keepdims=True) + jnp.exp(sink - m)
        o = (_dot(e.astype(BF16), vc_ref[...]) + _dot(em.astype(BF16), vm)) / l
        oc_ref[:, hs] = jnp.where(lo, o[:N_META], o[N_META:]).astype(BF16)


def _meta_queries(proj_r, proj_m, sink, lvecs, g_sub, abr, abm, cbr, cbm, lam_init, nb):
    lspec = _const_spec((1, HEAD_DIM), 1)
    org_rows = NA_ROWS * GRID_W
    out = jax.ShapeDtypeStruct((nb * N_META, BRANCH_W), BF16)
    out_spec = pl.BlockSpec((N_META, BRANCH_W), lambda b: (b, 0))
    return pl.pallas_call(
        functools.partial(_meta_kernel, lam_init=lam_init),
        grid=(nb,),
        in_specs=[
            pl.BlockSpec(memory_space=pltpu.SMEM),
            lspec, lspec, lspec, lspec,
            _const_spec((1, LANES), 1),
            pl.BlockSpec((N_META, IN_COLS), lambda b: (b, 0)),
            pl.BlockSpec((SEQ, 512), lambda b: (b, KA_BLK)),
            pl.BlockSpec((SEQ, 512), lambda b: (b, VA_BLK)),
            pl.BlockSpec((org_rows, 512), lambda b: (b * (SEQ // org_rows), KB_BLK)),
            pl.BlockSpec((org_rows, 512), lambda b: (b * (SEQ // org_rows), VB_BLK)),
            pl.BlockSpec((WINDOW, LANES), lambda b: (b * (SEQ // WINDOW), KC_BLK128)),
            pl.BlockSpec((WINDOW, LANES), lambda b: (b * (SEQ // WINDOW), VC_BLK128)),
            _const_spec((A_HEADS, N_META, SEQ), 1),
            _const_spec((A_HEADS, N_META, LANES), 1),
            _const_spec((C_HEADS, N_META, WINDOW), 1),
            _const_spec((C_HEADS, N_META, LANES), 1),
        ],
        out_specs=[out_spec, out_spec, out_spec],
        out_shape=[out, out, out],
        compiler_params=_cparams(1),
        name="meta_queries",
    )(sink, *lvecs, g_sub, proj_m, proj_r, proj_r, proj_r, proj_r, proj_r, proj_r, abr, abm, cbr, cbm)


def _t5_bucket_np(rel):
    nb = T5_BUCKETS // 2
    max_exact = nb // 2
    rel = np.asarray(rel, np.int64)
    n = np.abs(rel)
    n2 = np.maximum(n, 1) ** 2
    large = np.minimum(np.floor(np.log2(n2.astype(np.float64))).astype(np.int64) + 2, nb - 1)
    return (np.where(rel > 0, nb, 0) + np.where(n < max_exact, n, large)).astype(np.int32)


def _t5_vals(table, rel):
    idx = _t5_bucket_np(rel)
    v = jnp.take(table.astype(F32), jnp.asarray(idx.reshape(-1)), axis=0)
    return jnp.moveaxis(v, -1, 0).reshape((table.shape[1],) + idx.shape)


def _toeplitz(w, n, m):
    p = n + m - 1
    wp = jnp.concatenate([w, jnp.zeros(w.shape[:-1] + (1,), w.dtype)], axis=-1)
    flat = jnp.tile(wp, (1,) * (w.ndim - 1) + (n,))[..., :n * p]
    return flat.reshape(w.shape[:-1] + (n, p))[..., n - 1:n - 1 + m]


def _pad_lanes_neg(x):
    pad = jnp.full(x.shape[:-1] + (LANES - x.shape[-1],), NEG, F32)
    return jnp.concatenate([x, pad], axis=-1)


def _t5_tables(t5_table):
    ta = t5_table[:, :A_HEADS]
    tc = t5_table[:, A_HEADS:]
    out = {}
    rel = np.arange(TQ + TK - 1)[None, :] - (TQ - 1) + (np.arange(-2, 3) * TK)[:, None]
    out["a_bias5"] = _toeplitz(_t5_vals(ta, rel), TQ, TK)
    near = np.arange(N_META)[None, :] - (N_META + np.arange(TQ))[:, None]
    far = np.full((1, N_META), -(SEQ + N_META))

    def meta_key_bias(tab):
        first = _t5_vals(tab, near)
        rest = jnp.broadcast_to(_t5_vals(tab, far), (tab.shape[1], SEQ - TQ, N_META))
        return _pad_lanes_neg(jnp.concatenate([first, rest], axis=1))

    out["a_biasm"] = meta_key_bias(ta)
    out["c_biasm"] = meta_key_bias(tc)
    offs = np.array([0, -WINDOW, -(GQA_KW - TQ)])
    relw = np.arange(TQ + GQA_KW - 1)[None, :] - (TQ - 1) + offs[:, None]
    band = _toeplitz(_t5_vals(tc, relw), TQ, GQA_KW)
    relb = (np.arange(GQA_KW)[None, None, :] - np.arange(TQ)[None, :, None] + offs[:, None, None])
    band = jnp.where(jnp.asarray(np.abs(relb) <= WINDOW)[None], band, NEG)
    out["c_band"] = jnp.swapaxes(band, 0, 1)
    mq = np.arange(N_META)[:, None]
    near_k = N_META + np.arange(TK)[None, :] - mq
    far_k = np.full((1, 1), SEQ + N_META)
    first = _t5_vals(ta, near_k)
    rest = jnp.broadcast_to(_t5_vals(ta, far_k), (A_HEADS, N_META, SEQ - TK))
    out["a_mq_real"] = jnp.concatenate([first, rest], axis=-1)
    out["a_mq_meta"] = _pad_lanes_neg(_t5_vals(ta, np.arange(N_META)[None, :] - mq))
    rel0 = N_META + np.arange(WINDOW)[None, :] - mq
    out["c_mq_real"] = jnp.where(jnp.asarray(rel0 <= WINDOW)[None], _t5_vals(tc, rel0), NEG)
    out["c_mq_meta"] = _pad_lanes_neg(_t5_vals(tc, np.arange(N_META)[None, :] - mq))
    return out


def _nat_bias(rpb):
    n_rows = SEQ // GRID_W
    w = jnp.pad(rpb.astype(F32), ((0, 0), (0, 0), (GRID_W - NA_COLS, GRID_W - NA_COLS)))
    tiles = _toeplitz(w, GRID_W, GRID_W)
    cols = np.arange(GRID_W)
    cstart = np.clip(cols - NA_COLS // 2, 0, GRID_W - NA_COLS)
    col_ok = (cols[None, :] >= cstart[:, None]) & (cols[None, :] < cstart[:, None] + NA_COLS)
    tiles = jnp.where(jnp.asarray(col_ok)[None, None], tiles, NEG)
    neg_tile = jnp.full((B_HEADS, 1, GRID_W, GRID_W), NEG, F32)
    tiles = jnp.concatenate([tiles, neg_tile], axis=1)
    n_blk = n_rows // NAT_R
    idx = np.zeros((3, NAT_R, NAT_W), np.int32)
    for v, blk in enumerate((0, 1, n_blk - 1)):
        r0 = NAT_R * blk
        w0 = int(np.clip(r0 - NA_ROWS // 2, 0, n_rows - NAT_W))
        for qr in range(NAT_R):
            r = r0 + qr
            rs = int(np.clip(r - NA_ROWS // 2, 0, n_rows - NA_ROWS))
            for kr in range(NAT_W):
                krow = w0 + kr
                idx[v, qr, kr] = krow - r + NA_ROWS - 1 if rs <= krow < rs + NA_ROWS else 2 * NA_ROWS - 1
    g = jnp.take(tiles, jnp.asarray(idx.reshape(-1)), axis=1)
    g = g.reshape(B_HEADS, 3, NAT_R, NAT_W, GRID_W, GRID_W)
    g = jnp.transpose(g, (1, 0, 2, 4, 3, 5))
    return g.reshape(3, B_HEADS, NAT_R * GRID_W, NAT_W * GRID_W)


def _prep_w_in(w):
    col = np.arange(IN_COLS)
    is_q = (col < 512) | ((col >= 1536) & (col < 2048)) | ((col >= 3072) & (col < 3584))
    w = w * jnp.asarray(np.where(is_q, SCALE, 1.0), F32)[None, :]
    qc = w[:, 3072:3584].reshape(D_MODEL, C_HEADS, HEAD_DIM)[:, np.asarray(C_PERM)].reshape(D_MODEL, 512)
    w = jnp.concatenate([w[:, :3072], qc, w[:, 3584:]], axis=1)
    return w.astype(BF16)


def kernel(x, meta_tokens, t5_table, norm_ffn1, w_ffn1_in, w_ffn1_out, norm_mix, w_in, lambda_q1, lambda_k1,
           lambda_q2, lambda_k2, subln_gain, natten_rpb, sink_logits, w_branch, w_gate, w_out, norm_ffn2,
           w_ffn2_in, w_ffn2_out, final_norm):
    nb, seq, d = x.shape
    assert (seq, d) == (SEQ, D_MODEL)
    depth = norm_ffn1.shape[0]
    h_r = x.reshape(nb * SEQ, D_MODEL)
    h_m = jnp.broadcast_to(meta_tokens[None].astype(x.dtype), (nb, N_META, D_MODEL)).reshape(nb * N_META, D_MODEL)
    t5 = _t5_tables(t5_table)
    perm = np.asarray(C_PERM)
    out = None
    for l in range(depth):
        lam_init = 0.8 - 0.6 * math.exp(-0.3 * l)
        w1i, w1o = w_ffn1_in[l].astype(BF16), w_ffn1_out[l].astype(BF16)
        w2i, w2o = w_ffn2_in[l].astype(BF16), w_ffn2_out[l].astype(BF16)
        wi = _prep_w_in(w_in[l])
        wg = w_gate[l].astype(BF16)
        wb2 = w_branch[l, 2].reshape(C_HEADS, HEAD_DIM, D_MODEL)[perm].reshape(BRANCH_W, D_MODEL)
        wb = jnp.stack([w_branch[l, 0], w_branch[l, 1], wb2]).astype(BF16)
        wo = w_out[l].astype(BF16)
        lvecs = [v[l].reshape(1, HEAD_DIM).astype(F32) for v in (lambda_q1, lambda_k1, lambda_q2, lambda_k2)]
        g_sub = subln_gain[l].reshape(1, LANES).astype(F32)
        sink = sink_logits[l].astype(F32)[perm]
        nat_bias = _nat_bias(natten_rpb[l])

        h_r = _ffn(h_r, norm_ffn1[l], w1i, w1o)
        h_m = _ffn(h_m, norm_ffn1[l], w1i, w1o)

        p_r = _inproj(h_r, norm_mix[l], wi)
        p_m = _inproj(h_m, norm_mix[l], wi)
        ya_r = _diff_real(p_r, p_m, lvecs, g_sub, t5["a_bias5"], t5["a_biasm"], lam_init, nb)
        yb_r = _nat_real(p_r, p_m, nat_bias, nb)
        yc_r = _gqa_real(p_r, p_m, sink, t5["c_band"], t5["c_biasm"], nb)
        ya_m, yb_m, yc_m = _meta_queries(p_r, p_m, sink, lvecs, g_sub, t5["a_mq_real"], t5["a_mq_meta"],
                                         t5["c_mq_real"], t5["c_mq_meta"], lam_init, nb)
        h_r = _merge(h_r, norm_mix[l], ya_r, yb_r, yc_r, wg, wb, wo)
        h_m = _merge(h_m, norm_mix[l], ya_m, yb_m, yc_m, wg, wb, wo)

        last = l == depth - 1
        h_r = _ffn(h_r, norm_ffn2[l], w2i, w2o, final_gain=final_norm if last else None)
        h_m = _ffn(h_m, norm_ffn2[l], w2i, w2o)
        out = h_r
    return out.reshape(nb, SEQ, D_MODEL)
```

```python
import functools
import math

import numpy as np
import jax
import jax.numpy as jnp
from jax import lax
from jax.experimental import pallas as pl
from jax.experimental.pallas import tpu as pltpu

F32 = jnp.float32
BF16 = jnp.bfloat16

D_MODEL = 1024
SEQ = 4096
N_META = 16
GRID_W = 64
HEAD_DIM = 64
LANES = 128
A_HEADS = 4
B_HEADS = 8
C_HEADS = 8
C_KV_HEADS = 2
NA_ROWS = 8
NA_COLS = 16
WINDOW = 128
T5_BUCKETS = 32
D_FF = 2816
BRANCH_W = 512
IN_COLS = 3840
EPS = 1e-6
NEG = -1e30
SCALE = HEAD_DIM ** -0.5
LOG2E = math.log2(math.e)

QA_BLK, KA_BLK, VA_BLK, QB_BLK, KB_BLK, VB_BLK, QC_BLK = 0, 1, 2, 3, 4, 5, 6
KC_BLK128, VC_BLK128 = 28, 29
C_PERM = (0, 4, 1, 5, 2, 6, 3, 7)

TM_REAL = 512
FF_CHUNK = 256
PROJ_CHUNK = 768
TQ = 256
TK = 256
TKD = 512
VT_ROWS = 144
BIAS_TALL = 6 * TQ + TKD - TQ
NAT_R = 4
NAT_W = 12
GQA_KW = 512
VMEM_LIMIT = 56 * 1024 * 1024


def _cparams(n_axes):
    return pltpu.CompilerParams(dimension_semantics=("parallel",) * n_axes,
                                vmem_limit_bytes=VMEM_LIMIT)


def _const_spec(shape, n_grid):
    zeros = (0,) * len(shape)
    if n_grid == 1:
        return pl.BlockSpec(shape, lambda i: zeros)
    return pl.BlockSpec(shape, lambda i, j: zeros)


def _rms(x, g):
    return x * lax.rsqrt(jnp.mean(x * x, axis=-1, keepdims=True) + EPS) * g


def _dot(a, b):
    return jnp.dot(a, b, preferred_element_type=F32)


def _dot_t(a, b):
    return lax.dot_general(a, b, (((1,), (1,)), ((), ())), preferred_element_type=F32)


def _ffn_kernel(h_ref, g_ref, win_ref, wout_ref, *rest, final):
    if final:
        fg_ref, o_ref, act_ref = rest
    else:
        o_ref, act_ref = rest
    x = h_ref[...]
    xn = _rms(x, g_ref[...]).astype(BF16)
    for c in range(D_FF // FF_CHUNK):
        lo = c * FF_CHUNK
        gg = _dot(xn, win_ref[:, lo:lo + FF_CHUNK])
        uu = _dot(xn, win_ref[:, D_FF + lo:D_FF + lo + FF_CHUNK])
        act_ref[:, lo:lo + FF_CHUNK] = (gg * jax.nn.sigmoid(gg) * uu).astype(BF16)
    hn = x + 0.5 * _dot(act_ref[...], wout_ref[...])
    if final:
        hn = _rms(hn, fg_ref[...])
    o_ref[...] = hn


def _ffn(h, gain, w_in, w_out, final_gain=None):
    rows = h.shape[0]
    tm = min(TM_REAL, rows)
    final = final_gain is not None
    in_specs = [
        pl.BlockSpec((tm, D_MODEL), lambda i: (i, 0)),
        _const_spec((1, D_MODEL), 1),
        _const_spec((D_MODEL, 2 * D_FF), 1),
        _const_spec((D_FF, D_MODEL), 1),
    ]
    args = [h, gain.reshape(1, D_MODEL), w_in, w_out]
    if final:
        in_specs.append(_const_spec((1, D_MODEL), 1))
        args.append(final_gain.reshape(1, D_MODEL))
    return pl.pallas_call(
        functools.partial(_ffn_kernel, final=final),
        grid=(rows // tm,),
        in_specs=in_specs,
        out_specs=pl.BlockSpec((tm, D_MODEL), lambda i: (i, 0)),
        out_shape=jax.ShapeDtypeStruct((rows, D_MODEL), F32),
        scratch_shapes=[pltpu.VMEM((tm, D_FF), BF16)],
        compiler_params=_cparams(1),
        name="ffn",
    )(*args)


def _inproj_kernel(h_ref, g_ref, w_ref, *rest, transposed):
    if transposed:
        wqt_ref, wvt_ref, o_ref, qt_ref, vt_ref = rest
    else:
        (o_ref,) = rest
    xn = _rms(h_ref[...], g_ref[...]).astype(BF16)
    for c in range(IN_COLS // PROJ_CHUNK):
        lo = c * PROJ_CHUNK
        o_ref[:, lo:lo + PROJ_CHUNK] = _dot(xn, w_ref[:, lo:lo + PROJ_CHUNK]).astype(BF16)
    if transposed:
        tm = xn.shape[0]
        qt = _dot_t(wqt_ref[...], xn).astype(BF16)
        for c in range(tm // TQ):
            qt_ref[c] = qt[:, c * TQ:(c + 1) * TQ]
        vt = _dot_t(wvt_ref[...], xn).astype(BF16)
        for h in range(A_HEADS):
            vt_ref[0, h * VT_ROWS:h * VT_ROWS + LANES, :] = vt[h * LANES:(h + 1) * LANES]
            vt_ref[0, h * VT_ROWS + LANES:(h + 1) * VT_ROWS, :] = jnp.ones((VT_ROWS - LANES, tm), BF16)


def _inproj(h, gain, w_in, wqt=None, wvt=None):
    rows = h.shape[0]
    tm = min(TM_REAL, rows)
    transposed = wqt is not None
    in_specs = [
        pl.BlockSpec((tm, D_MODEL), lambda i: (i, 0)),
        _const_spec((1, D_MODEL), 1),
        _const_spec((D_MODEL, IN_COLS), 1),
    ]
    args = [h, gain.reshape(1, D_MODEL), w_in]
    out_specs = [pl.BlockSpec((tm, IN_COLS), lambda i: (i, 0))]
    out_shape = [jax.ShapeDtypeStruct((rows, IN_COLS), BF16)]
    if transposed:
        assert tm == TKD
        in_specs += [_const_spec((512, D_MODEL), 1), _const_spec((512, D_MODEL), 1)]
        args += [wqt, wvt]
        out_specs += [pl.BlockSpec((tm // TQ, 512, TQ), lambda i: (i, 0, 0)),
                      pl.BlockSpec((1, A_HEADS * VT_ROWS, tm), lambda i: (i, 0, 0))]
        out_shape += [jax.ShapeDtypeStruct((rows // TQ, 512, TQ), BF16),
                      jax.ShapeDtypeStruct((rows // tm, A_HEADS * VT_ROWS, tm), BF16)]
    res = pl.pallas_call(
        functools.partial(_inproj_kernel, transposed=transposed),
        grid=(rows // tm,),
        in_specs=in_specs,
        out_specs=out_specs,
        out_shape=out_shape,
        compiler_params=_cparams(1),
        name="inproj",
    )(*args)
    return res if transposed else res[0]


def _merge_kernel(h_ref, g_ref, ya_ref, yb_ref, yc_ref, wg_ref, wb_ref, wo_ref, o_ref):
    x = h_ref[...]
    xn = _rms(x, g_ref[...]).astype(BF16)
    merged = None
    for i, y_ref in enumerate((ya_ref, yb_ref, yc_ref)):
        term = jax.nn.sigmoid(_dot(xn, wg_ref[i])) * _dot(y_ref[...], wb_ref[i])
        merged = term if merged is None else merged + term
    o_ref[...] = x + _dot(merged.astype(BF16), wo_ref[...])


def _merge(h, gain, ya, yb, yc, w_gate, w_branch, w_out):
    rows = h.shape[0]
    tm = min(TM_REAL, rows)
    row_spec = lambda w: pl.BlockSpec((tm, w), lambda i: (i, 0))
    return pl.pallas_call(
        _merge_kernel,
        grid=(rows // tm,),
        in_specs=[
            row_spec(D_MODEL),
            _const_spec((1, D_MODEL), 1),
            row_spec(BRANCH_W), row_spec(BRANCH_W), row_spec(BRANCH_W),
            _const_spec((3, D_MODEL, D_MODEL), 1),
            _const_spec((3, BRANCH_W, D_MODEL), 1),
            _const_spec((D_MODEL, D_MODEL), 1),
        ],
        out_specs=row_spec(D_MODEL),
        out_shape=jax.ShapeDtypeStruct((rows, D_MODEL), F32),
        compiler_params=_cparams(1),
        name="merge",
    )(h, gain.reshape(1, D_MODEL), ya, yb, yc, w_gate, w_branch, w_out)


def _lane_lo(rows):
    return lax.broadcasted_iota(jnp.int32, (rows, LANES), 1) < HEAD_DIM


def _split_halves(q):
    lo = _lane_lo(q.shape[0])
    zero = jnp.zeros_like(q)
    return jnp.concatenate([jnp.where(lo, q, zero), jnp.where(lo, zero, q)], axis=0)


def _pad_rows(x, rows):
    return jnp.concatenate([x, jnp.zeros((rows - x.shape[0], x.shape[1]), x.dtype)], axis=0)


def _twice(b):
    return jnp.concatenate([b, b], axis=0)


def _lam(lq1_ref, lk1_ref, lq2_ref, lk2_ref, lam_init):
    s1 = jnp.sum(lq1_ref[...] * lk1_ref[...], axis=-1, keepdims=True)
    s2 = jnp.sum(lq2_ref[...] * lk2_ref[...], axis=-1, keepdims=True)
    return jnp.exp(s1) - jnp.exp(s2) + lam_init


def _diff_finish(o, lam, g, lam_init):
    m = o.shape[0] // 2
    d = o[:m] - lam * o[m:]
    return _rms(d, g) * (1.0 - lam_init)


def _diff_real_kernel(lq1_ref, lk1_ref, lq2_ref, lk2_ref, g_ref, qt_ref, k_ref, vt_ref, km_ref, vmt_ref,
                      bias_ref, biasm_ref, o_ref, s0_ref, s1_ref, *, lam_init):
    i = pl.program_id(1)
    lam = _lam(lq1_ref, lk1_ref, lq2_ref, lk2_ref, lam_init)
    g = g_ref[...]
    nk = k_ref.shape[0] // TKD
    top = lax.broadcasted_iota(jnp.int32, (LANES, TQ), 0) < HEAD_DIM
    for h in range(A_HEADS):
        hs = slice(h * LANES, (h + 1) * LANES)
        vrows = slice(h * VT_ROWS, (h + 1) * VT_ROWS)
        qt = qt_ref[0, hs, :]
        zero = jnp.zeros_like(qt)
        q12t = jnp.concatenate([jnp.where(top, qt, zero), jnp.where(top, zero, qt)], axis=1)
        bm = biasm_ref[h]
        s = _dot(km_ref[:, hs], q12t) + jnp.concatenate([bm, bm], axis=1)
        m0 = jnp.max(s, axis=0, keepdims=True)
        acc0 = _dot(vmt_ref[0, vrows, :], jnp.exp2(s - m0).astype(BF16))

        def scores(j, dst_ref, hs=hs, h=h, q12t=q12t):
            kt = k_ref[pl.ds(pl.multiple_of(j * TKD, TKD), TKD), hs]
            e = jnp.clip(2 * j - i, -3, 2) + 3
            bt = bias_ref[h, pl.ds(pl.multiple_of(e * TQ, TQ), TKD), :]
            dst_ref[...] = _dot(kt, q12t) + jnp.concatenate([bt, bt], axis=1)

        def fold(j, src_ref, m, acc, vrows=vrows):
            s = src_ref[...]
            m_new = jnp.maximum(m, jnp.max(s, axis=0, keepdims=True))
            alpha = jnp.exp2(m - m_new)
            p = jnp.exp2(s - m_new).astype(BF16)
            return m_new, alpha * acc + _dot(vt_ref[j, vrows, :], p)

        scores(0, s0_ref)

        def pair(jj, carry):
            m, acc = carry
            scores(2 * jj + 1, s1_ref)
            m, acc = fold(2 * jj, s0_ref, m, acc)
            scores(2 * jj + 2, s0_ref)
            return fold(2 * jj + 1, s1_ref, m, acc)

        m, acc = lax.fori_loop(0, nk // 2 - 1, pair, (m0, acc0))
        scores(nk - 1, s1_ref)
        m, acc = fold(nk - 2, s0_ref, m, acc)
        m, acc = fold(nk - 1, s1_ref, m, acc)
        o = acc[:LANES] / acc[LANES:LANES + 1]
        d = o[:, :TQ] - lam * o[:, TQ:]
        y = d * lax.rsqrt(jnp.mean(d * d, axis=0, keepdims=True) + EPS) * g * (1.0 - lam_init)
        o_ref[:, hs] = y.T.astype(BF16)


def _diff_real(qt, proj_r, vt, proj_m, vmt, lvecs, g_col, bias_tall, biasm_t, lam_init, nb):
    nq = SEQ // TQ
    lspec = _const_spec((1, HEAD_DIM), 2)
    return pl.pallas_call(
        functools.partial(_diff_real_kernel, lam_init=lam_init),
        grid=(nb, nq),
        in_specs=[
            lspec, lspec, lspec, lspec,
            _const_spec((LANES, 1), 2),
            pl.BlockSpec((1, 512, TQ), lambda b, i: (b * nq + i, 0, 0)),
            pl.BlockSpec((SEQ, 512), lambda b, i: (b, KA_BLK)),
            pl.BlockSpec((SEQ // TKD, A_HEADS * VT_ROWS, TKD), lambda b, i: (b, 0, 0)),
            pl.BlockSpec((N_META, 512), lambda b, i: (b, KA_BLK)),
            pl.BlockSpec((1, A_HEADS * VT_ROWS, N_META), lambda b, i: (b, 0, 0)),
            _const_spec((A_HEADS, BIAS_TALL, TQ), 2),
            pl.BlockSpec((A_HEADS, N_META, TQ), lambda b, i: (0, 0, i)),
        ],
        out_specs=pl.BlockSpec((TQ, 512), lambda b, i: (b * nq + i, 0)),
        out_shape=jax.ShapeDtypeStruct((nb * SEQ, BRANCH_W), BF16),
        scratch_shapes=[pltpu.VMEM((TKD, 2 * TQ), F32), pltpu.VMEM((TKD, 2 * TQ), F32)],
        compiler_params=_cparams(2),
        name="diff_real",
    )(*lvecs, g_col, qt, proj_r, vt, proj_m, vmt, bias_tall, biasm_t)


def _nat_real_kernel(q_ref, k_ref, v_ref, km_ref, vm_ref, bias_ref, o_ref):
    blk = pl.program_id(1)
    n_rows = k_ref.shape[0] // GRID_W
    w0 = jnp.clip(NAT_R * blk - NA_ROWS // 2, 0, n_rows - NAT_W)
    start = pl.multiple_of(w0 * GRID_W, GRID_W)
    nkeys = NAT_W * GRID_W
    pad_bias = jnp.where(lax.broadcasted_iota(jnp.int32, (1, LANES), 1) < N_META, 0.0, NEG)
    lo = _lane_lo(TQ)
    for pr in range(B_HEADS // 2):
        hs = slice(pr * LANES, (pr + 1) * LANES)
        q2 = _split_halves(q_ref[:, hs])
        kw = k_ref[pl.ds(start, nkeys), hs]
        vw = v_ref[pl.ds(start, nkeys), hs]
        km = _pad_rows(km_ref[:, hs], LANES)
        vm = _pad_rows(vm_ref[:, hs], LANES)
        s = _dot_t(q2, kw) + jnp.concatenate([bias_ref[0, 2 * pr], bias_ref[0, 2 * pr + 1]], axis=0)
        sm = _dot_t(q2, km) + pad_bias
        m = jnp.maximum(jnp.max(s, axis=-1, keepdims=True), jnp.max(sm, axis=-1, keepdims=True))
        e = jnp.exp(s - m)
        em = jnp.exp(sm - m)
        l = jnp.sum(e, axis=-1, keepdims=True) + jnp.sum(em, axis=-1, keepdims=True)
        o = (_dot(e.astype(BF16), vw) + _dot(em.astype(BF16), vm)) / l
        o_ref[:, hs] = jnp.where(lo, o[:TQ], o[TQ:]).astype(BF16)


def _nat_real(proj_r, proj_m, bias, nb):
    nq = SEQ // TQ
    nkeys = NAT_W * GRID_W

    def variant(b, i):
        return (jnp.where(i == 0, 0, jnp.where(i == nq - 1, 2, 1)), 0, 0, 0)

    return pl.pallas_call(
        _nat_real_kernel,
        grid=(nb, nq),
        in_specs=[
            pl.BlockSpec((TQ, 512), lambda b, i: (b * nq + i, QB_BLK)),
            pl.BlockSpec((SEQ, 512), lambda b, i: (b, KB_BLK)),
            pl.BlockSpec((SEQ, 512), lambda b, i: (b, VB_BLK)),
            pl.BlockSpec((N_META, 512), lambda b, i: (b, KB_BLK)),
            pl.BlockSpec((N_META, 512), lambda b, i: (b, VB_BLK)),
            pl.BlockSpec((1, B_HEADS, TQ, nkeys), variant),
        ],
        out_specs=pl.BlockSpec((TQ, 512), lambda b, i: (b * nq + i, 0)),
        out_shape=jax.ShapeDtypeStruct((nb * SEQ, BRANCH_W), BF16),
        compiler_params=_cparams(2),
        name="nat_real",
    )(proj_r, proj_r, proj_r, proj_m, proj_m, bias)


def _gqa_real_kernel(sink_ref, q_ref, k_ref, v_ref, km_ref, vm_ref, band_ref, mb_ref, o_ref):
    t = pl.program_id(1)
    start = pl.multiple_of(jnp.clip(TQ * t - WINDOW, 0, k_ref.shape[0] - GQA_KW), WINDOW)
    kw = k_ref[pl.ds(start, GQA_KW), :]
    vw = v_ref[pl.ds(start, GQA_KW), :]
    km = _pad_rows(km_ref[...], LANES)
    vm = _pad_rows(vm_ref[...], LANES)
    lo = _lane_lo(TQ)
    npair = C_HEADS // 2
    for pr in range(npair):
        hs = slice(pr * LANES, (pr + 1) * LANES)
        q2 = _split_halves(q_ref[:, hs])
        s = _dot_t(q2, kw) + jnp.concatenate([band_ref[0, pr], band_ref[0, npair + pr]], axis=0)
        sm = _dot_t(q2, km) + jnp.concatenate([mb_ref[pr], mb_ref[npair + pr]], axis=0)
        sink = jnp.concatenate([jnp.full((TQ, 1), sink_ref[pr], F32),
                                jnp.full((TQ, 1), sink_ref[npair + pr], F32)], axis=0)
        m = jnp.maximum(jnp.maximum(jnp.max(s, axis=-1, keepdims=True), jnp.max(sm, axis=-1, keepdims=True)), sink)
        e = jnp.exp(s - m)
        em = jnp.exp(sm - m)
        l = jnp.sum(e, axis=-1, keepdims=True) + jnp.sum(em, axis=-1, keepdims=True) + jnp.exp(sink - m)
        o = (_dot(e.astype(BF16), vw) + _dot(em.astype(BF16), vm)) / l
        o_ref[:, hs] = jnp.where(lo, o[:TQ], o[TQ:]).astype(BF16)


def _gqa_real(proj_r, proj_m, sink, band, mbias, nb):
    nq = SEQ // TQ

    def variant(b, i):
        return (jnp.where(i == 0, 0, jnp.where(i == nq - 1, 2, 1)), 0, 0, 0)

    return pl.pallas_call(
        _gqa_real_kernel,
        grid=(nb, nq),
        in_specs=[
            pl.BlockSpec(memory_space=pltpu.SMEM),
            pl.BlockSpec((TQ, 512), lambda b, i: (b * nq + i, QC_BLK)),
            pl.BlockSpec((SEQ, LANES), lambda b, i: (b, KC_BLK128)),
            pl.BlockSpec((SEQ, LANES), lambda b, i: (b, VC_BLK128)),
            pl.BlockSpec((N_META, LANES), lambda b, i: (b, KC_BLK128)),
            pl.BlockSpec((N_META, LANES), lambda b, i: (b, VC_BLK128)),
            pl.BlockSpec((1, C_HEADS, TQ, GQA_KW), variant),
            pl.BlockSpec((C_HEADS, TQ, LANES), lambda b, i: (0, i, 0)),
        ],
        out_specs=pl.BlockSpec((TQ, 512), lambda b, i: (b * nq + i, 0)),
        out_shape=jax.ShapeDtypeStruct((nb * SEQ, BRANCH_W), BF16),
        compiler_params=_cparams(2),
        name="gqa_real",
    )(sink, proj_r, proj_r, proj_r, proj_m, proj_m, band, mbias)


def _meta_kernel(sink_ref, lq1_ref, lk1_ref, lq2_ref, lk2_ref, g_ref, pm_ref, ka_ref, va_ref, kb_ref, vb_ref,
                 kc_ref, vc_ref, abr_ref, abm_ref, cbr_ref, cbm_ref, oa_ref, ob_ref, oc_ref, *, lam_init):
    lam = _lam(lq1_ref, lk1_ref, lq2_ref, lk2_ref, lam_init)
    g = g_ref[...]
    lo = _lane_lo(N_META)
    lane = lax.broadcasted_iota(jnp.int32, (1, LANES), 1)
    pad_bias = jnp.where(lane < N_META, 0.0, NEG)

    def col(blk512, sub):
        base = blk512 * 512 + sub * LANES
        return slice(base, base + LANES)

    for h in range(A_HEADS):
        hs = slice(h * LANES, (h + 1) * LANES)
        q12 = _split_halves(pm_ref[:, col(QA_BLK, h)])
        km = _pad_rows(pm_ref[:, col(KA_BLK, h)], LANES)
        vm = _pad_rows(pm_ref[:, col(VA_BLK, h)], LANES)
        s = _dot_t(q12, ka_ref[:, hs]) + _twice(abr_ref[h])
        sm = _dot_t(q12, km) + _twice(abm_ref[h])
        m = jnp.maximum(jnp.max(s, axis=-1, keepdims=True), jnp.max(sm, axis=-1, keepdims=True))
        e = jnp.exp(s - m)
        em = jnp.exp(sm - m)
        l = jnp.sum(e, axis=-1, keepdims=True) + jnp.sum(em, axis=-1, keepdims=True)
        o = (_dot(e.astype(BF16), va_ref[:, hs]) + _dot(em.astype(BF16), vm)) / l
        oa_ref[:, hs] = _diff_finish(o, lam, g, lam_init).astype(BF16)

    org = jnp.bitwise_and(lax.broadcasted_iota(jnp.int32, (1, NA_ROWS * GRID_W), 1), GRID_W - 1) < NA_COLS
    org_bias = jnp.where(org, 0.0, NEG)
    for pr in range(B_HEADS // 2):
        hs = slice(pr * LANES, (pr + 1) * LANES)
        q2 = _split_halves(pm_ref[:, col(QB_BLK, pr)])
        km = _pad_rows(pm_ref[:, col(KB_BLK, pr)], LANES)
        vm = _pad_rows(pm_ref[:, col(VB_BLK, pr)], LANES)
        s = _dot_t(q2, kb_ref[:, hs]) + org_bias
        sm = _dot_t(q2, km) + pad_bias
        m = jnp.maximum(jnp.max(s, axis=-1, keepdims=True), jnp.max(sm, axis=-1, keepdims=True))
        e = jnp.exp(s - m)
        em = jnp.exp(sm - m)
        l = jnp.sum(e, axis=-1, keepdims=True) + jnp.sum(em, axis=-1, keepdims=True)
        o = (_dot(e.astype(BF16), vb_ref[:, hs]) + _dot(em.astype(BF16), vm)) / l
        ob_ref[:, hs] = jnp.where(lo, o[:N_META], o[N_META:]).astype(BF16)

    kc_base = KC_BLK128 * LANES
    vc_base = VC_BLK128 * LANES
    km = _pad_rows(pm_ref[:, kc_base:kc_base + LANES], LANES)
    vm = _pad_rows(pm_ref[:, vc_base:vc_base + LANES], LANES)
    npair = C_HEADS // 2
    for pr in range(npair):
        hs = slice(pr * LANES, (pr + 1) * LANES)
        q2 = _split_halves(pm_ref[:, col(QC_BLK, pr)])
        s = _dot_t(q2, kc_ref[...]) + jnp.concatenate([cbr_ref[pr], cbr_ref[npair + pr]], axis=0)
        sm = _dot_t(q2, km) + jnp.concatenate([cbm_ref[pr], cbm_ref[npair + pr]], axis=0)
        sink = jnp.concatenate([jnp.full((N_META, 1), sink_ref[pr], F32),
                                jnp.full((N_META, 1), sink_ref[npair + pr], F32)], axis=0)
        m = jnp.maximum(jnp.maximum(jnp.max(s, axis=-1, keepdims=True), jnp.max(sm, axis=-1, keepdims=True)), sink)
        e = jnp.exp(s - m)
        em = jnp.exp(sm - m)
        l = jnp.sum(e, axis=-1, keepdims=True) + jnp.sum(em, axis=-1, keepdims=True) + jnp.exp(sink - m)
        o = (_dot(e.astype(BF16), vc_ref[...]) + _dot(em.astype(BF16), vm)) / l
        oc_ref[:, hs] = jnp.where(lo, o[:N_META], o[N_META:]).astype(BF16)


def _meta_queries(proj_r, proj_m, sink, lvecs, g_sub, abr, abm, cbr, cbm, lam_init, nb):
    lspec = _const_spec((1, HEAD_DIM), 1)
    org_rows = NA_ROWS * GRID_W
    out = jax.ShapeDtypeStruct((nb * N_META, BRANCH_W), BF16)
    out_spec = pl.BlockSpec((N_META, BRANCH_W), lambda b: (b, 0))
    return pl.pallas_call(
        functools.partial(_meta_kernel, lam_init=lam_init),
        grid=(nb,),
        in_specs=[
            pl.BlockSpec(memory_space=pltpu.SMEM),
            lspec, lspec, lspec, lspec,
            _const_spec((1, LANES), 1),
            pl.BlockSpec((N_META, IN_COLS), lambda b: (b, 0)),
            pl.BlockSpec((SEQ, 512), lambda b: (b, KA_BLK)),
            pl.BlockSpec((SEQ, 512), lambda b: (b, VA_BLK)),
            pl.BlockSpec((org_rows, 512), lambda b: (b * (SEQ // org_rows), KB_BLK)),
            pl.BlockSpec((org_rows, 512), lambda b: (b * (SEQ // org_rows), VB_BLK)),
            pl.BlockSpec((WINDOW, LANES), lambda b: (b * (SEQ // WINDOW), KC_BLK128)),
            pl.BlockSpec((WINDOW, LANES), lambda b: (b * (SEQ // WINDOW), VC_BLK128)),
            _const_spec((A_HEADS, N_META, SEQ), 1),
            _const_spec((A_HEADS, N_META, LANES), 1),
            _const_spec((C_HEADS, N_META, WINDOW), 1),
            _const_spec((C_HEADS, N_META, LANES), 1),
        ],
        out_specs=[out_spec, out_spec, out_spec],
        out_shape=[out, out, out],
        compiler_params=_cparams(1),
        name="meta_queries",
    )(sink, *lvecs, g_sub, proj_m, proj_r, proj_r, proj_r, proj_r, proj_r, proj_r, abr, abm, cbr, cbm)


def _t5_bucket_np(rel):
    nb = T5_BUCKETS // 2
    max_exact = nb // 2
    rel = np.asarray(rel, np.int64)
    n = np.abs(rel)
    n2 = np.maximum(n, 1) ** 2
    large = np.minimum(np.floor(np.log2(n2.astype(np.float64))).astype(np.int64) + 2, nb - 1)
    return (np.where(rel > 0, nb, 0) + np.where(n < max_exact, n, large)).astype(np.int32)


def _t5_vals(table, rel):
    idx = _t5_bucket_np(rel)
    v = jnp.take(table.astype(F32), jnp.asarray(idx.reshape(-1)), axis=0)
    return jnp.moveaxis(v, -1, 0).reshape((table.shape[1],) + idx.shape)


def _toeplitz(w, n, m):
    p = n + m - 1
    wp = jnp.concatenate([w, jnp.zeros(w.shape[:-1] + (1,), w.dtype)], axis=-1)
    flat = jnp.tile(wp, (1,) * (w.ndim - 1) + (n,))[..., :n * p]
    return flat.reshape(w.shape[:-1] + (n, p))[..., n - 1:n - 1 + m]


def _pad_lanes_neg(x):
    pad = jnp.full(x.shape[:-1] + (LANES - x.shape[-1],), NEG, F32)
    return jnp.concatenate([x, pad], axis=-1)


def _t5_tables(t5_table):
    ta = t5_table[:, :A_HEADS]
    tc = t5_table[:, A_HEADS:]
    out = {}
    rel = (BIAS_TALL - 1 - 3 * TQ) - np.arange(BIAS_TALL + TQ - 1)
    out["a_tall"] = _toeplitz(_t5_vals(ta, rel), BIAS_TALL, TQ) * LOG2E
    rel_t = np.arange(N_META)[:, None] - (N_META + np.arange(TQ))[None, :]
    first_t = _t5_vals(ta, rel_t)
    rest_t = jnp.broadcast_to(_t5_vals(ta, np.full((1, 1), -(SEQ + N_META))), (A_HEADS, N_META, SEQ - TQ))
    out["a_biasm_t"] = jnp.concatenate([first_t, rest_t], axis=-1) * LOG2E
    near = np.arange(N_META)[None, :] - (N_META + np.arange(TQ))[:, None]
    far = np.full((1, N_META), -(SEQ + N_META))

    def meta_key_bias(tab):
        first = _t5_vals(tab, near)
        rest = jnp.broadcast_to(_t5_vals(tab, far), (tab.shape[1], SEQ - TQ, N_META))
        return _pad_lanes_neg(jnp.concatenate([first, rest], axis=1))

    out["c_biasm"] = meta_key_bias(tc)
    offs = np.array([0, -WINDOW, -(GQA_KW - TQ)])
    relw = np.arange(TQ + GQA_KW - 1)[None, :] - (TQ - 1) + offs[:, None]
    band = _toeplitz(_t5_vals(tc, relw), TQ, GQA_KW)
    relb = (np.arange(GQA_KW)[None, None, :] - np.arange(TQ)[None, :, None] + offs[:, None, None])
    band = jnp.where(jnp.asarray(np.abs(relb) <= WINDOW)[None], band, NEG)
    out["c_band"] = jnp.swapaxes(band, 0, 1)
    mq = np.arange(N_META)[:, None]
    near_k = N_META + np.arange(TK)[None, :] - mq
    far_k = np.full((1, 1), SEQ + N_META)
    first = _t5_vals(ta, near_k)
    rest = jnp.broadcast_to(_t5_vals(ta, far_k), (A_HEADS, N_META, SEQ - TK))
    out["a_mq_real"] = jnp.concatenate([first, rest], axis=-1)
    out["a_mq_meta"] = _pad_lanes_neg(_t5_vals(ta, np.arange(N_META)[None, :] - mq))
    rel0 = N_META + np.arange(WINDOW)[None, :] - mq
    out["c_mq_real"] = jnp.where(jnp.asarray(rel0 <= WINDOW)[None], _t5_vals(tc, rel0), NEG)
    out["c_mq_meta"] = _pad_lanes_neg(_t5_vals(tc, np.arange(N_META)[None, :] - mq))
    return out


def _nat_bias(rpb):
    n_rows = SEQ // GRID_W
    w = jnp.pad(rpb.astype(F32), ((0, 0), (0, 0), (GRID_W - NA_COLS, GRID_W - NA_COLS)))
    tiles = _toeplitz(w, GRID_W, GRID_W)
    cols = np.arange(GRID_W)
    cstart = np.clip(cols - NA_COLS // 2, 0, GRID_W - NA_COLS)
    col_ok = (cols[None, :] >= cstart[:, None]) & (cols[None, :] < cstart[:, None] + NA_COLS)
    tiles = jnp.where(jnp.asarray(col_ok)[None, None], tiles, NEG)
    neg_tile = jnp.full((B_HEADS, 1, GRID_W, GRID_W), NEG, F32)
    tiles = jnp.concatenate([tiles, neg_tile], axis=1)
    n_blk = n_rows // NAT_R
    idx = np.zeros((3, NAT_R, NAT_W), np.int32)
    for v, blk in enumerate((0, 1, n_blk - 1)):
        r0 = NAT_R * blk
        w0 = int(np.clip(r0 - NA_ROWS // 2, 0, n_rows - NAT_W))
        for qr in range(NAT_R):
            r = r0 + qr
            rs = int(np.clip(r - NA_ROWS // 2, 0, n_rows - NA_ROWS))
            for kr in range(NAT_W):
                krow = w0 + kr
                idx[v, qr, kr] = krow - r + NA_ROWS - 1 if rs <= krow < rs + NA_ROWS else 2 * NA_ROWS - 1
    g = jnp.take(tiles, jnp.asarray(idx.reshape(-1)), axis=1)
    g = g.reshape(B_HEADS, 3, NAT_R, NAT_W, GRID_W, GRID_W)
    g = jnp.transpose(g, (1, 0, 2, 4, 3, 5))
    return g.reshape(3, B_HEADS, NAT_R * GRID_W, NAT_W * GRID_W)


def _prep_w_in(w):
    col = np.arange(IN_COLS)
    is_q = (col < 512) | ((col >= 1536) & (col < 2048)) | ((col >= 3072) & (col < 3584))
    w = w * jnp.asarray(np.where(is_q, SCALE, 1.0), F32)[None, :]
    qc = w[:, 3072:3584].reshape(D_MODEL, C_HEADS, HEAD_DIM)[:, np.asarray(C_PERM)].reshape(D_MODEL, 512)
    w = jnp.concatenate([w[:, :3072], qc, w[:, 3584:]], axis=1)
    return w.astype(BF16)


def kernel(x, meta_tokens, t5_table, norm_ffn1, w_ffn1_in, w_ffn1_out, norm_mix, w_in, lambda_q1, lambda_k1,
           lambda_q2, lambda_k2, subln_gain, natten_rpb, sink_logits, w_branch, w_gate, w_out, norm_ffn2,
           w_ffn2_in, w_ffn2_out, final_norm):
    nb, seq, d = x.shape
    assert (seq, d) == (SEQ, D_MODEL)
    depth = norm_ffn1.shape[0]
    h_r = x.reshape(nb * SEQ, D_MODEL)
    h_m = jnp.broadcast_to(meta_tokens[None].astype(x.dtype), (nb, N_META, D_MODEL)).reshape(nb * N_META, D_MODEL)
    t5 = _t5_tables(t5_table)
    perm = np.asarray(C_PERM)
    out = None
    for l in range(depth):
        lam_init = 0.8 - 0.6 * math.exp(-0.3 * l)
        w1i, w1o = w_ffn1_in[l].astype(BF16), w_ffn1_out[l].astype(BF16)
        w2i, w2o = w_ffn2_in[l].astype(BF16), w_ffn2_out[l].astype(BF16)
        wi = _prep_w_in(w_in[l])
        wg = w_gate[l].astype(BF16)
        wb2 = w_branch[l, 2].reshape(C_HEADS, HEAD_DIM, D_MODEL)[perm].reshape(BRANCH_W, D_MODEL)
        wb = jnp.stack([w_branch[l, 0], w_branch[l, 1], wb2]).astype(BF16)
        wo = w_out[l].astype(BF16)
        lvecs = [v[l].reshape(1, HEAD_DIM).astype(F32) for v in (lambda_q1, lambda_k1, lambda_q2, lambda_k2)]
        g_sub = subln_gain[l].reshape(1, LANES).astype(F32)
        sink = sink_logits[l].astype(F32)[perm]
        nat_bias = _nat_bias(natten_rpb[l])

        h_r = _ffn(h_r, norm_ffn1[l], w1i, w1o)
        h_m = _ffn(h_m, norm_ffn1[l], w1i, w1o)

        wqt = (w_in[l][:, :512] * (SCALE * LOG2E)).astype(BF16).T
        p_r, qt_r, vt_r = _inproj(h_r, norm_mix[l], wi, wqt, wi[:, 1024:1536].T)
        p_m = _inproj(h_m, norm_mix[l], wi)
        vm_t = jnp.swapaxes(p_m[:, 1024:1536].reshape(nb, N_META, A_HEADS, LANES), 1, 3)
        vm_t = jnp.swapaxes(vm_t, 1, 2)
        ones = jnp.ones((nb, A_HEADS, VT_ROWS - LANES, N_META), BF16)
        vm_t = jnp.concatenate([vm_t, ones], axis=2).reshape(nb, A_HEADS * VT_ROWS, N_META)
        ya_r = _diff_real(qt_r, p_r, vt_r, p_m, vm_t, lvecs, g_sub.reshape(LANES, 1), t5["a_tall"],
                          t5["a_biasm_t"], lam_init, nb)
        yb_r = _nat_real(p_r, p_m, nat_bias, nb)
        yc_r = _gqa_real(p_r, p_m, sink, t5["c_band"], t5["c_biasm"], nb)
        ya_m, yb_m, yc_m = _meta_queries(p_r, p_m, sink, lvecs, g_sub, t5["a_mq_real"], t5["a_mq_meta"],
                                         t5["c_mq_real"], t5["c_mq_meta"], lam_init, nb)
        h_r = _merge(h_r, norm_mix[l], ya_r, yb_r, yc_r, wg, wb, wo)
        h_m = _merge(h_m, norm_mix[l], ya_m, yb_m, yc_m, wg, wb, wo)

        last = l == depth - 1
        h_r = _ffn(h_r, norm_ffn2[l], w2i, w2o, final_gain=final_norm if last else None)
        h_m = _ffn(h_m, norm_ffn2[l], w2i, w2o)
        out = h_r
    return out.reshape(nb, SEQ, D_MODEL)
```

```python
import functools
import math

import numpy as np
import jax
import jax.numpy as jnp
from jax import lax
from jax.experimental import pallas as pl
from jax.experimental.pallas import tpu as pltpu

F32 = jnp.float32
BF16 = jnp.bfloat16

D_MODEL = 1024
SEQ = 4096
N_META = 16
GRID_W = 64
HEAD_DIM = 64
LANES = 128
A_HEADS = 4
B_HEADS = 8
C_HEADS = 8
C_KV_HEADS = 2
NA_ROWS = 8
NA_COLS = 16
WINDOW = 128
T5_BUCKETS = 32
D_FF = 2816
BRANCH_W = 512
IN_COLS = 3840
EPS = 1e-6
NEG = -1e30
SCALE = HEAD_DIM ** -0.5
LOG2E = math.log2(math.e)

QA_BLK, KA_BLK, VA_BLK, QB_BLK, KB_BLK, VB_BLK, QC_BLK = 0, 1, 2, 3, 4, 5, 6
KC_BLK128, VC_BLK128 = 28, 29
C_PERM = (0, 4, 1, 5, 2, 6, 3, 7)

TM_REAL = 1024
FF_CHUNK = 256
PROJ_CHUNK = 768
TQ = 256
TK = 256
TKD = 512
LOOKAHEAD = 2
VT_ROWS = 144
VT_GROUPS = (A_HEADS, B_HEADS // 2, 1)
VT_CHUNK = (TKD, 256, 128)
BIAS_TALL = 6 * TQ + TKD - TQ
NAT_R = 4
NAT_W = 12
GQA_KW = 512
VMEM_LIMIT = 56 * 1024 * 1024


def _cparams(n_axes):
    return pltpu.CompilerParams(dimension_semantics=("parallel",) * n_axes,
                                vmem_limit_bytes=VMEM_LIMIT)


def _const_spec(shape, n_grid):
    zeros = (0,) * len(shape)
    if n_grid == 1:
        return pl.BlockSpec(shape, lambda i: zeros, pipeline_mode=pl.Buffered(1))
    return pl.BlockSpec(shape, lambda i, j: zeros, pipeline_mode=pl.Buffered(1))


def _rms(x, g):
    return x * lax.rsqrt(jnp.mean(x * x, axis=-1, keepdims=True) + EPS) * g


def _dot(a, b):
    return jnp.dot(a, b, preferred_element_type=F32)


def _dot_t(a, b):
    return lax.dot_general(a, b, (((1,), (1,)), ((), ())), preferred_element_type=F32)


def _ffn_kernel(h_ref, g_ref, win_ref, wout_ref, *rest, final):
    if final:
        fg_ref, o_ref, act_ref = rest
    else:
        o_ref, act_ref = rest
    x = h_ref[...]
    xn = _rms(x, g_ref[...]).astype(BF16)
    for c in range(D_FF // FF_CHUNK):
        lo = c * FF_CHUNK
        gg = _dot(xn, win_ref[:, lo:lo + FF_CHUNK])
        uu = _dot(xn, win_ref[:, D_FF + lo:D_FF + lo + FF_CHUNK])
        act_ref[:, lo:lo + FF_CHUNK] = (gg * jax.nn.sigmoid(gg) * uu).astype(BF16)
    hn = x + 0.5 * _dot(act_ref[...], wout_ref[...])
    if final:
        hn = _rms(hn, fg_ref[...])
    o_ref[...] = hn


def _ffn(h, gain, w_in, w_out, final_gain=None):
    rows = h.shape[0]
    tm = min(TM_REAL, rows)
    final = final_gain is not None
    in_specs = [
        pl.BlockSpec((tm, D_MODEL), lambda i: (i, 0)),
        _const_spec((1, D_MODEL), 1),
        _const_spec((D_MODEL, 2 * D_FF), 1),
        _const_spec((D_FF, D_MODEL), 1),
    ]
    args = [h, gain.reshape(1, D_MODEL), w_in, w_out]
    if final:
        in_specs.append(_const_spec((1, D_MODEL), 1))
        args.append(final_gain.reshape(1, D_MODEL))
    return pl.pallas_call(
        functools.partial(_ffn_kernel, final=final),
        grid=(rows // tm,),
        in_specs=in_specs,
        out_specs=pl.BlockSpec((tm, D_MODEL), lambda i: (i, 0)),
        out_shape=jax.ShapeDtypeStruct((rows, D_MODEL), F32),
        scratch_shapes=[pltpu.VMEM((tm, D_FF), BF16)],
        compiler_params=_cparams(1),
        name="ffn",
    )(*args)


def _inproj_kernel(h_ref, g_ref, w_ref, *rest, transposed):
    if transposed:
        wvt_ref, o_ref, *vt_refs = rest
    else:
        (o_ref,) = rest
    xn = _rms(h_ref[...], g_ref[...]).astype(BF16)
    for c in range(IN_COLS // PROJ_CHUNK):
        lo = c * PROJ_CHUNK
        o_ref[:, lo:lo + PROJ_CHUNK] = _dot(xn, w_ref[:, lo:lo + PROJ_CHUNK]).astype(BF16)
    if transposed:
        tm = xn.shape[0]
        vt = _dot_t(wvt_ref[...], xn).astype(BF16)
        base = 0
        for vt_ref, groups, chunk in zip(vt_refs, VT_GROUPS, VT_CHUNK):
            for c in range(tm // chunk):
                cols = slice(c * chunk, (c + 1) * chunk)
                for g in range(groups):
                    vt_ref[c, g * VT_ROWS:g * VT_ROWS + LANES, :] = vt[base + g * LANES:base + (g + 1) * LANES, cols]
                    vt_ref[c, g * VT_ROWS + LANES:(g + 1) * VT_ROWS, :] = jnp.ones((VT_ROWS - LANES, chunk), BF16)
            base += groups * LANES


def _inproj(h, gain, w_in, wvt=None):
    rows = h.shape[0]
    tm = min(TM_REAL, rows)
    transposed = wvt is not None
    in_specs = [
        pl.BlockSpec((tm, D_MODEL), lambda i: (i, 0)),
        _const_spec((1, D_MODEL), 1),
        _const_spec((D_MODEL, IN_COLS), 1),
    ]
    args = [h, gain.reshape(1, D_MODEL), w_in]
    out_specs = [pl.BlockSpec((tm, IN_COLS), lambda i: (i, 0))]
    out_shape = [jax.ShapeDtypeStruct((rows, IN_COLS), BF16)]
    if transposed:
        in_specs.append(_const_spec(wvt.shape, 1))
        args.append(wvt)
        for groups, chunk in zip(VT_GROUPS, VT_CHUNK):
            out_specs.append(pl.BlockSpec((tm // chunk, groups * VT_ROWS, chunk), lambda i: (i, 0, 0)))
            out_shape.append(jax.ShapeDtypeStruct((rows // chunk, groups * VT_ROWS, chunk), BF16))
    res = pl.pallas_call(
        functools.partial(_inproj_kernel, transposed=transposed),
        grid=(rows // tm,),
        in_specs=in_specs,
        out_specs=out_specs,
        out_shape=out_shape,
        compiler_params=_cparams(1),
        name="inproj",
    )(*args)
    return res if transposed else res[0]


def _merge_kernel(h_ref, g_ref, ya_ref, yb_ref, yc_ref, wg_ref, wb_ref, wo_ref, o_ref):
    x = h_ref[...]
    xn = _rms(x, g_ref[...]).astype(BF16)
    merged = None
    for i, y_ref in enumerate((ya_ref, yb_ref, yc_ref)):
        term = jax.nn.sigmoid(_dot(xn, wg_ref[i])) * _dot(y_ref[...], wb_ref[i])
        merged = term if merged is None else merged + term
    o_ref[...] = x + _dot(merged.astype(BF16), wo_ref[...])


def _merge(h, gain, ya, yb, yc, w_gate, w_branch, w_out):
    rows = h.shape[0]
    tm = min(TM_REAL, rows)
    row_spec = lambda w: pl.BlockSpec((tm, w), lambda i: (i, 0))
    return pl.pallas_call(
        _merge_kernel,
        grid=(rows // tm,),
        in_specs=[
            row_spec(D_MODEL),
            _const_spec((1, D_MODEL), 1),
            row_spec(BRANCH_W), row_spec(BRANCH_W), row_spec(BRANCH_W),
            _const_spec((3, D_MODEL, D_MODEL), 1),
            _const_spec((3, BRANCH_W, D_MODEL), 1),
            _const_spec((D_MODEL, D_MODEL), 1),
        ],
        out_specs=row_spec(D_MODEL),
        out_shape=jax.ShapeDtypeStruct((rows, D_MODEL), F32),
        compiler_params=_cparams(1),
        name="merge",
    )(h, gain.reshape(1, D_MODEL), ya, yb, yc, w_gate, w_branch, w_out)


def _lane_lo(rows):
    return lax.broadcasted_iota(jnp.int32, (rows, LANES), 1) < HEAD_DIM


def _split_halves(q):
    lo = _lane_lo(q.shape[0])
    zero = jnp.zeros_like(q)
    return jnp.concatenate([jnp.where(lo, q, zero), jnp.where(lo, zero, q)], axis=0)


def _pad_rows(x, rows):
    return jnp.concatenate([x, jnp.zeros((rows - x.shape[0], x.shape[1]), x.dtype)], axis=0)


def _twice(b):
    return jnp.concatenate([b, b], axis=0)


def _side_by_side(a, b):
    return jnp.concatenate([a, b], axis=1)


def _lam(lq1_ref, lk1_ref, lq2_ref, lk2_ref, lam_init):
    s1 = jnp.sum(lq1_ref[...] * lk1_ref[...], axis=-1, keepdims=True)
    s2 = jnp.sum(lq2_ref[...] * lk2_ref[...], axis=-1, keepdims=True)
    return jnp.exp(s1) - jnp.exp(s2) + lam_init


def _diff_finish(o, lam, g, lam_init):
    m = o.shape[0] // 2
    d = o[:m] - lam * o[m:]
    return _rms(d, g) * (1.0 - lam_init)


def _pipeline(n, issue, consume, scr):
    pending = {}
    for t in range(min(LOOKAHEAD, n)):
        pending[t] = issue(t, scr[t % len(scr)])
    for t in range(n):
        if t + LOOKAHEAD < n:
            pending[t + LOOKAHEAD] = issue(t + LOOKAHEAD, scr[(t + LOOKAHEAD) % len(scr)])
        consume(t, scr[t % len(scr)], pending.pop(t))


def _pair_out(o):
    top = lax.broadcasted_iota(jnp.int32, (LANES, TQ), 0) < HEAD_DIM
    return jnp.where(top, o[:, :TQ], o[:, TQ:]).T


def _diff_real_kernel(lq1_ref, lk1_ref, lq2_ref, lk2_ref, g_ref, q_ref, k_ref, vt_ref, km_ref, vmt_ref,
                      bias_ref, biasm_ref, o_ref, *scr, lam_init):
    i = pl.program_id(1)
    lam = _lam(lq1_ref, lk1_ref, lq2_ref, lk2_ref, lam_init)
    g = g_ref[...]
    nk = k_ref.shape[0] // TKD

    def head_setup(h):
        hs = slice(h * LANES, (h + 1) * LANES)
        q12 = _split_halves(q_ref[:, hs])
        s = _dot_t(km_ref[:, hs], q12) + _side_by_side(biasm_ref[h], biasm_ref[h])
        m0 = jnp.max(s, axis=0, keepdims=True)
        p = jnp.exp2(s - m0).astype(BF16)
        acc0 = _dot(vmt_ref[0, h * VT_ROWS:(h + 1) * VT_ROWS, :], p)
        return q12, m0, acc0

    def scores(h, j, q12, dst_ref):
        kt = k_ref[j * TKD:(j + 1) * TKD, h * LANES:(h + 1) * LANES]
        e = jnp.clip(2 * j - i, -3, 2) + 3
        bt = bias_ref[h, pl.ds(pl.multiple_of(e * TQ, TQ), TKD), :]
        s = _dot_t(kt, q12) + _side_by_side(bt, bt)
        dst_ref[...] = s
        return jnp.max(s, axis=0, keepdims=True)

    def fold(h, j, src_ref, smax, m, acc):
        m_new = jnp.maximum(m, smax)
        alpha = jnp.exp2(m - m_new)
        p = jnp.exp2(src_ref[...] - m_new).astype(BF16)
        return m_new, alpha * acc + _dot(vt_ref[j, h * VT_ROWS:(h + 1) * VT_ROWS, :], p)

    def finish(h, acc):
        o = acc[:LANES] / acc[LANES:LANES + 1]
        d = o[:, :TQ] - lam * o[:, TQ:]
        y = d * lax.rsqrt(jnp.mean(d * d, axis=0, keepdims=True) + EPS) * g * (1.0 - lam_init)
        o_ref[:, h * LANES:(h + 1) * LANES] = y.T.astype(BF16)

    tiles = [(h, j) for h in range(A_HEADS) for j in range(nk)]
    setup = {}
    smax = {}

    def issue(t):
        h, j = tiles[t]
        if h not in setup:
            setup[h] = head_setup(h)
        smax[t] = scores(h, j, setup[h][0], scr[t % len(scr)])

    for t in range(LOOKAHEAD):
        issue(t)
    m = acc = None
    for t, (h, j) in enumerate(tiles):
        if t + LOOKAHEAD < len(tiles):
            issue(t + LOOKAHEAD)
        if j == 0:
            _, m, acc = setup[h]
        m, acc = fold(h, j, scr[t % len(scr)], smax.pop(t), m, acc)
        if j == nk - 1:
            finish(h, acc)


def _diff_real(proj_r, vt, proj_m, vmt, lvecs, g_col, bias_tall, biasm_t, lam_init, nb):
    nq = SEQ // TQ
    lspec = _const_spec((1, HEAD_DIM), 2)
    return pl.pallas_call(
        functools.partial(_diff_real_kernel, lam_init=lam_init),
        grid=(nb, nq),
        in_specs=[
            lspec, lspec, lspec, lspec,
            _const_spec((LANES, 1), 2),
            pl.BlockSpec((TQ, 512), lambda b, i: (b * nq + i, QA_BLK)),
            pl.BlockSpec((SEQ, 512), lambda b, i: (b, KA_BLK)),
            pl.BlockSpec((SEQ // TKD, A_HEADS * VT_ROWS, TKD), lambda b, i: (b, 0, 0)),
            pl.BlockSpec((N_META, 512), lambda b, i: (b, KA_BLK)),
            pl.BlockSpec((1, A_HEADS * VT_ROWS, N_META), lambda b, i: (b, 0, 0)),
            _const_spec((A_HEADS, BIAS_TALL, TQ), 2),
            pl.BlockSpec((A_HEADS, N_META, TQ), lambda b, i: (0, 0, i)),
        ],
        out_specs=pl.BlockSpec((TQ, 512), lambda b, i: (b * nq + i, 0)),
        out_shape=jax.ShapeDtypeStruct((nb * SEQ, BRANCH_W), BF16),
        scratch_shapes=[pltpu.VMEM((TKD, 2 * TQ), F32)] * (LOOKAHEAD + 1),
        compiler_params=_cparams(2),
        name="diff_real",
    )(*lvecs, g_col, proj_r, proj_r, vt, proj_m, vmt, bias_tall, biasm_t)


def _nat_real_kernel(q_ref, k_ref, vt_ref, km_ref, vmt_ref, bias_ref, o_ref, *scr):
    blk = pl.program_id(1)
    chunk = vt_ref.shape[2]
    n_chunks = NAT_W * GRID_W // chunk
    c0 = jnp.clip(blk - 1, 0, vt_ref.shape[0] - n_chunks)
    start = pl.multiple_of(c0 * chunk, chunk)

    def scores(pr, dst_ref):
        hs = slice(pr * LANES, (pr + 1) * LANES)
        q2 = _split_halves(q_ref[:, hs])
        kw = k_ref[pl.ds(start, n_chunks * chunk), hs]
        s = _dot_t(kw, q2) + _side_by_side(bias_ref[0, 2 * pr], bias_ref[0, 2 * pr + 1])
        dst_ref[...] = s
        sm = _dot_t(km_ref[:, hs], q2)
        return sm, jnp.maximum(jnp.max(s, axis=0, keepdims=True), jnp.max(sm, axis=0, keepdims=True))

    def finish(pr, src_ref, pending):
        sm, m = pending
        rows = slice(pr * VT_ROWS, (pr + 1) * VT_ROWS)
        acc = _dot(vmt_ref[0, rows, :], jnp.exp2(sm - m).astype(BF16))
        for c in range(n_chunks):
            p = jnp.exp2(src_ref[c * chunk:(c + 1) * chunk, :] - m).astype(BF16)
            acc = acc + _dot(vt_ref[c0 + c, rows, :], p)
        o_ref[:, pr * LANES:(pr + 1) * LANES] = _pair_out(acc[:LANES] / acc[LANES:LANES + 1]).astype(BF16)

    _pipeline(B_HEADS // 2, scores, finish, scr)


def _nat_real(proj_r, vt, proj_m, vmt, bias_t, nb):
    nq = SEQ // TQ
    nkeys = NAT_W * GRID_W
    chunk = VT_CHUNK[1]

    def variant(b, i):
        return (jnp.where(i == 0, 0, jnp.where(i == nq - 1, 2, 1)), 0, 0, 0)

    return pl.pallas_call(
        _nat_real_kernel,
        grid=(nb, nq),
        in_specs=[
            pl.BlockSpec((TQ, 512), lambda b, i: (b * nq + i, QB_BLK)),
            pl.BlockSpec((SEQ, 512), lambda b, i: (b, KB_BLK)),
            pl.BlockSpec((SEQ // chunk, VT_GROUPS[1] * VT_ROWS, chunk), lambda b, i: (b, 0, 0)),
            pl.BlockSpec((N_META, 512), lambda b, i: (b, KB_BLK)),
            pl.BlockSpec((1, VT_GROUPS[1] * VT_ROWS, N_META), lambda b, i: (b, 0, 0)),
            pl.BlockSpec((1, B_HEADS, nkeys, TQ), variant),
        ],
        out_specs=pl.BlockSpec((TQ, 512), lambda b, i: (b * nq + i, 0)),
        out_shape=jax.ShapeDtypeStruct((nb * SEQ, BRANCH_W), BF16),
        scratch_shapes=[pltpu.VMEM((nkeys, 2 * TQ), F32)] * (LOOKAHEAD + 1),
        compiler_params=_cparams(2),
        name="nat_real",
    )(proj_r, proj_r, vt, proj_m, vmt, bias_t)


def _gqa_real_kernel(sink_ref, q_ref, k_ref, vt_ref, km_ref, vmt_ref, band_ref, mb_ref, o_ref, *scr):
    t = pl.program_id(1)
    chunk = vt_ref.shape[2]
    n_chunks = GQA_KW // chunk
    c0 = jnp.clip((TQ // chunk) * t - WINDOW // chunk, 0, vt_ref.shape[0] - n_chunks)
    kw = k_ref[pl.ds(pl.multiple_of(c0 * chunk, chunk), GQA_KW), :]
    vts = [_side_by_side(vt_ref[c0 + 2 * c], vt_ref[c0 + 2 * c + 1]) for c in range(n_chunks // 2)]
    npair = C_HEADS // 2

    def scores(pr, dst_ref):
        q2 = _split_halves(q_ref[:, pr * LANES:(pr + 1) * LANES])
        s = _dot_t(kw, q2) + _side_by_side(band_ref[0, pr], band_ref[0, npair + pr])
        dst_ref[...] = s
        sm = _dot_t(km_ref[...], q2) + _side_by_side(mb_ref[pr], mb_ref[npair + pr])
        sink = _side_by_side(jnp.full((1, TQ), sink_ref[pr], F32), jnp.full((1, TQ), sink_ref[npair + pr], F32))
        m = jnp.maximum(jnp.maximum(jnp.max(s, axis=0, keepdims=True), jnp.max(sm, axis=0, keepdims=True)), sink)
        return sm, sink, m

    def finish(pr, src_ref, pending):
        sm, sink, m = pending
        acc = _dot(vmt_ref[0], jnp.exp2(sm - m).astype(BF16))
        for c, vt2 in enumerate(vts):
            p = jnp.exp2(src_ref[2 * c * chunk:2 * (c + 1) * chunk, :] - m).astype(BF16)
            acc = acc + _dot(vt2, p)
        l = acc[LANES:LANES + 1] + jnp.exp2(sink - m)
        o_ref[:, pr * LANES:(pr + 1) * LANES] = _pair_out(acc[:LANES] / l).astype(BF16)

    _pipeline(npair, scores, finish, scr)


def _gqa_real(proj_r, vt, proj_m, vmt, sink, band_t, mbias_t, nb):
    nq = SEQ // TQ
    chunk = VT_CHUNK[2]

    def variant(b, i):
        return (jnp.where(i == 0, 0, jnp.where(i == nq - 1, 2, 1)), 0, 0, 0)

    return pl.pallas_call(
        _gqa_real_kernel,
        grid=(nb, nq),
        in_specs=[
            pl.BlockSpec(memory_space=pltpu.SMEM),
            pl.BlockSpec((TQ, 512), lambda b, i: (b * nq + i, QC_BLK)),
            pl.BlockSpec((SEQ, LANES), lambda b, i: (b, KC_BLK128)),
            pl.BlockSpec((SEQ // chunk, VT_ROWS, chunk), lambda b, i: (b, 0, 0)),
            pl.BlockSpec((N_META, LANES), lambda b, i: (b, KC_BLK128)),
            pl.BlockSpec((1, VT_ROWS, N_META), lambda b, i: (b, 0, 0)),
            pl.BlockSpec((1, C_HEADS, GQA_KW, TQ), variant),
            pl.BlockSpec((C_HEADS, N_META, TQ), lambda b, i: (0, 0, i)),
        ],
        out_specs=pl.BlockSpec((TQ, 512), lambda b, i: (b * nq + i, 0)),
        out_shape=jax.ShapeDtypeStruct((nb * SEQ, BRANCH_W), BF16),
        scratch_shapes=[pltpu.VMEM((GQA_KW, 2 * TQ), F32)] * (LOOKAHEAD + 1),
        compiler_params=_cparams(2),
        name="gqa_real",
    )(sink, proj_r, proj_r, vt, proj_m, vmt, band_t, mbias_t)


def _meta_kernel(sink_ref, lq1_ref, lk1_ref, lq2_ref, lk2_ref, g_ref, pm_ref, ka_ref, va_ref, kb_ref, vb_ref,
                 kc_ref, vc_ref, abr_ref, abm_ref, cbr_ref, cbm_ref, oa_ref, ob_ref, oc_ref, *, lam_init):
    lam = _lam(lq1_ref, lk1_ref, lq2_ref, lk2_ref, lam_init)
    g = g_ref[...]
    lo = _lane_lo(N_META)
    lane = lax.broadcasted_iota(jnp.int32, (1, LANES), 1)
    pad_bias = jnp.where(lane < N_META, 0.0, NEG)

    def col(blk512, sub):
        base = blk512 * 512 + sub * LANES
        return slice(base, base + LANES)

    for h in range(A_HEADS):
        hs = slice(h * LANES, (h + 1) * LANES)
        q12 = _split_halves(pm_ref[:, col(QA_BLK, h)])
        km = _pad_rows(pm_ref[:, col(KA_BLK, h)], LANES)
        vm = _pad_rows(pm_ref[:, col(VA_BLK, h)], LANES)
        s = _dot_t(q12, ka_ref[:, hs]) + _twice(abr_ref[h])
        sm = _dot_t(q12, km) + _twice(abm_ref[h])
        m = jnp.maximum(jnp.max(s, axis=-1, keepdims=True), jnp.max(sm, axis=-1, keepdims=True))
        e = jnp.exp2(s - m)
        em = jnp.exp2(sm - m)
        l = jnp.sum(e, axis=-1, keepdims=True) + jnp.sum(em, axis=-1, keepdims=True)
        o = (_dot(e.astype(BF16), va_ref[:, hs]) + _dot(em.astype(BF16), vm)) / l
        oa_ref[:, hs] = _diff_finish(o, lam, g, lam_init).astype(BF16)

    org = jnp.bitwise_and(lax.broadcasted_iota(jnp.int32, (1, NA_ROWS * GRID_W), 1), GRID_W - 1) < NA_COLS
    org_bias = jnp.where(org, 0.0, NEG)
    for pr in range(B_HEADS // 2):
        hs = slice(pr * LANES, (pr + 1) * LANES)
        q2 = _split_halves(pm_ref[:, col(QB_BLK, pr)])
        km = _pad_rows(pm_ref[:, col(KB_BLK, pr)], LANES)
        vm = _pad_rows(pm_ref[:, col(VB_BLK, pr)], LANES)
        s = _dot_t(q2, kb_ref[:, hs]) + org_bias
        sm = _dot_t(q2, km) + pad_bias
        m = jnp.maximum(jnp.max(s, axis=-1, keepdims=True), jnp.max(sm, axis=-1, keepdims=True))
        e = jnp.exp2(s - m)
        em = jnp.exp2(sm - m)
        l = jnp.sum(e, axis=-1, keepdims=True) + jnp.sum(em, axis=-1, keepdims=True)
        o = (_dot(e.astype(BF16), vb_ref[:, hs]) + _dot(em.astype(BF16), vm)) / l
        ob_ref[:, hs] = jnp.where(lo, o[:N_META], o[N_META:]).astype(BF16)

    kc_base = KC_BLK128 * LANES
    vc_base = VC_BLK128 * LANES
    km = _pad_rows(pm_ref[:, kc_base:kc_base + LANES], LANES)
    vm = _pad_rows(pm_ref[:, vc_base:vc_base + LANES], LANES)
    npair = C_HEADS // 2
    for pr in range(npair):
        hs = slice(pr * LANES, (pr + 1) * LANES)
        q2 = _split_halves(pm_ref[:, col(QC_BLK, pr)])
        s = _dot_t(q2, kc_ref[...]) + jnp.concatenate([cbr_ref[pr], cbr_ref[npair + pr]], axis=0)
        sm = _dot_t(q2, km) + jnp.concatenate([cbm_ref[pr], cbm_ref[npair + pr]], axis=0)
        sink = jnp.concatenate([jnp.full((N_META, 1), sink_ref[pr], F32),
                                jnp.full((N_META, 1), sink_ref[npair + pr], F32)], axis=0)
        m = jnp.maximum(jnp.maximum(jnp.max(s, axis=-1, keepdims=True), jnp.max(sm, axis=-1, keepdims=True)), sink)
        e = jnp.exp2(s - m)
        em = jnp.exp2(sm - m)
        l = jnp.sum(e, axis=-1, keepdims=True) + jnp.sum(em, axis=-1, keepdims=True) + jnp.exp2(sink - m)
        o = (_dot(e.astype(BF16), vc_ref[...]) + _dot(em.astype(BF16), vm)) / l
        oc_ref[:, hs] = jnp.where(lo, o[:N_META], o[N_META:]).astype(BF16)


def _meta_queries(proj_r, proj_m, sink, lvecs, g_sub, abr, abm, cbr, cbm, lam_init, nb):
    lspec = _const_spec((1, HEAD_DIM), 1)
    org_rows = NA_ROWS * GRID_W
    out = jax.ShapeDtypeStruct((nb * N_META, BRANCH_W), BF16)
    out_spec = pl.BlockSpec((N_META, BRANCH_W), lambda b: (b, 0))
    return pl.pallas_call(
        functools.partial(_meta_kernel, lam_init=lam_init),
        grid=(nb,),
        in_specs=[
            pl.BlockSpec(memory_space=pltpu.SMEM),
            lspec, lspec, lspec, lspec,
            _const_spec((1, LANES), 1),
            pl.BlockSpec((N_META, IN_COLS), lambda b: (b, 0)),
            pl.BlockSpec((SEQ, 512), lambda b: (b, KA_BLK)),
            pl.BlockSpec((SEQ, 512), lambda b: (b, VA_BLK)),
            pl.BlockSpec((org_rows, 512), lambda b: (b * (SEQ // org_rows), KB_BLK)),
            pl.BlockSpec((org_rows, 512), lambda b: (b * (SEQ // org_rows), VB_BLK)),
            pl.BlockSpec((WINDOW, LANES), lambda b: (b * (SEQ // WINDOW), KC_BLK128)),
            pl.BlockSpec((WINDOW, LANES), lambda b: (b * (SEQ // WINDOW), VC_BLK128)),
            _const_spec((A_HEADS, N_META, SEQ), 1),
            _const_spec((A_HEADS, N_META, LANES), 1),
            _const_spec((C_HEADS, N_META, WINDOW), 1),
            _const_spec((C_HEADS, N_META, LANES), 1),
        ],
        out_specs=[out_spec, out_spec, out_spec],
        out_shape=[out, out, out],
        compiler_params=_cparams(1),
        name="meta_queries",
    )(sink, *lvecs, g_sub, proj_m, proj_r, proj_r, proj_r, proj_r, proj_r, proj_r, abr, abm, cbr, cbm)


def _t5_bucket_np(rel):
    nb = T5_BUCKETS // 2
    max_exact = nb // 2
    rel = np.asarray(rel, np.int64)
    n = np.abs(rel)
    n2 = np.maximum(n, 1) ** 2
    large = np.minimum(np.floor(np.log2(n2.astype(np.float64))).astype(np.int64) + 2, nb - 1)
    return (np.where(rel > 0, nb, 0) + np.where(n < max_exact, n, large)).astype(np.int32)


def _t5_vals(table, rel):
    idx = _t5_bucket_np(rel)
    v = jnp.take(table.astype(F32), jnp.asarray(idx.reshape(-1)), axis=0)
    return jnp.moveaxis(v, -1, 0).reshape((table.shape[1],) + idx.shape)


def _toeplitz(w, n, m):
    p = n + m - 1
    wp = jnp.concatenate([w, jnp.zeros(w.shape[:-1] + (1,), w.dtype)], axis=-1)
    flat = jnp.tile(wp, (1,) * (w.ndim - 1) + (n,))[..., :n * p]
    return flat.reshape(w.shape[:-1] + (n, p))[..., n - 1:n - 1 + m]


def _pad_lanes_neg(x):
    pad = jnp.full(x.shape[:-1] + (LANES - x.shape[-1],), NEG, F32)
    return jnp.concatenate([x, pad], axis=-1)


def _t5_tables(t5_table):
    table = t5_table.astype(F32) * LOG2E
    ta = table[:, :A_HEADS]
    tc = table[:, A_HEADS:]
    out = {}
    rel = (BIAS_TALL - 1 - 3 * TQ) - np.arange(BIAS_TALL + TQ - 1)
    out["a_tall"] = _toeplitz(_t5_vals(ta, rel), BIAS_TALL, TQ)
    rel_t = np.arange(N_META)[:, None] - (N_META + np.arange(TQ))[None, :]
    far_t = np.full((1, 1), -(SEQ + N_META))

    def meta_key_bias_t(tab):
        rest = jnp.broadcast_to(_t5_vals(tab, far_t), (tab.shape[1], N_META, SEQ - TQ))
        return jnp.concatenate([_t5_vals(tab, rel_t), rest], axis=-1)

    out["a_biasm_t"] = meta_key_bias_t(ta)
    out["c_biasm_t"] = meta_key_bias_t(tc)
    offs = np.array([0, -WINDOW, -(GQA_KW - TQ)])
    relw = (GQA_KW - 1) - np.arange(GQA_KW + TQ - 1)[None, :] + offs[:, None]
    band = _toeplitz(_t5_vals(tc, relw), GQA_KW, TQ)
    relb = (np.arange(GQA_KW)[None, :, None] - np.arange(TQ)[None, None, :] + offs[:, None, None])
    band = jnp.where(jnp.asarray(np.abs(relb) <= WINDOW)[None], band, NEG)
    out["c_band_t"] = jnp.swapaxes(band, 0, 1)
    mq = np.arange(N_META)[:, None]
    near_k = N_META + np.arange(TK)[None, :] - mq
    far_k = np.full((1, 1), SEQ + N_META)
    first = _t5_vals(ta, near_k)
    rest = jnp.broadcast_to(_t5_vals(ta, far_k), (A_HEADS, N_META, SEQ - TK))
    out["a_mq_real"] = jnp.concatenate([first, rest], axis=-1)
    out["a_mq_meta"] = _pad_lanes_neg(_t5_vals(ta, np.arange(N_META)[None, :] - mq))
    rel0 = N_META + np.arange(WINDOW)[None, :] - mq
    out["c_mq_real"] = jnp.where(jnp.asarray(rel0 <= WINDOW)[None], _t5_vals(tc, rel0), NEG)
    out["c_mq_meta"] = _pad_lanes_neg(_t5_vals(tc, np.arange(N_META)[None, :] - mq))
    return out


def _nat_bias_t(rpb):
    n_rows = SEQ // GRID_W
    w = jnp.pad(rpb.astype(F32) * LOG2E, ((0, 0), (0, 0), (GRID_W - NA_COLS, GRID_W - NA_COLS)))
    tiles = _toeplitz(w, GRID_W, GRID_W)
    cols = np.arange(GRID_W)
    cstart = np.clip(cols - NA_COLS // 2, 0, GRID_W - NA_COLS)
    col_ok = (cols[None, :] >= cstart[:, None]) & (cols[None, :] < cstart[:, None] + NA_COLS)
    tiles = jnp.where(jnp.asarray(col_ok)[None, None], tiles, NEG)
    neg_tile = jnp.full((B_HEADS, 1, GRID_W, GRID_W), NEG, F32)
    tiles = jnp.concatenate([tiles, neg_tile], axis=1)
    n_blk = n_rows // NAT_R
    idx = np.zeros((3, NAT_R, NAT_W), np.int32)
    for v, blk in enumerate((0, 1, n_blk - 1)):
        r0 = NAT_R * blk
        w0 = int(np.clip(r0 - NA_ROWS // 2, 0, n_rows - NAT_W))
        for qr in range(NAT_R):
            r = r0 + qr
            rs = int(np.clip(r - NA_ROWS // 2, 0, n_rows - NA_ROWS))
            for kr in range(NAT_W):
                krow = w0 + kr
                idx[v, qr, kr] = krow - r + NA_ROWS - 1 if rs <= krow < rs + NA_ROWS else 2 * NA_ROWS - 1
    g = jnp.take(tiles, jnp.asarray(idx.reshape(-1)), axis=1)
    g = g.reshape(B_HEADS, 3, NAT_R, NAT_W, GRID_W, GRID_W)
    g = jnp.transpose(g, (1, 0, 3, 5, 2, 4))
    return g.reshape(3, B_HEADS, NAT_W * GRID_W, NAT_R * GRID_W)


def _prep_w_in(w):
    col = np.arange(IN_COLS)
    is_q = (col < 512) | ((col >= 1536) & (col < 2048)) | ((col >= 3072) & (col < 3584))
    w = w * jnp.asarray(np.where(is_q, SCALE * LOG2E, 1.0), F32)[None, :]
    qc = w[:, 3072:3584].reshape(D_MODEL, C_HEADS, HEAD_DIM)[:, np.asarray(C_PERM)].reshape(D_MODEL, 512)
    w = jnp.concatenate([w[:, :3072], qc, w[:, 3584:]], axis=1)
    return w.astype(BF16)


def _meta_vt(p_m, lo, groups, nb):
    v = p_m[:, lo:lo + groups * LANES].reshape(nb, N_META, groups, LANES)
    v = jnp.transpose(v, (0, 2, 3, 1))
    ones = jnp.ones((nb, groups, VT_ROWS - LANES, N_META), BF16)
    return jnp.concatenate([v, ones], axis=2).reshape(nb, groups * VT_ROWS, N_META)


def kernel(x, meta_tokens, t5_table, norm_ffn1, w_ffn1_in, w_ffn1_out, norm_mix, w_in, lambda_q1, lambda_k1,
           lambda_q2, lambda_k2, subln_gain, natten_rpb, sink_logits, w_branch, w_gate, w_out, norm_ffn2,
           w_ffn2_in, w_ffn2_out, final_norm):
    nb, seq, d = x.shape
    assert (seq, d) == (SEQ, D_MODEL)
    depth = norm_ffn1.shape[0]
    h_r = x.reshape(nb * SEQ, D_MODEL)
    h_m = jnp.broadcast_to(meta_tokens[None].astype(x.dtype), (nb, N_META, D_MODEL)).reshape(nb * N_META, D_MODEL)
    t5 = _t5_tables(t5_table)
    perm = np.asarray(C_PERM)
    v_cols = np.concatenate([np.arange(1024, 1536), np.arange(2560, 3072), np.arange(3712, 3840)])
    out = None
    for l in range(depth):
        lam_init = 0.8 - 0.6 * math.exp(-0.3 * l)
        w1i, w1o = w_ffn1_in[l].astype(BF16), w_ffn1_out[l].astype(BF16)
        w2i, w2o = w_ffn2_in[l].astype(BF16), w_ffn2_out[l].astype(BF16)
        wi = _prep_w_in(w_in[l])
        wvt = wi[:, v_cols].T
        wg = w_gate[l].astype(BF16)
        wb2 = w_branch[l, 2].reshape(C_HEADS, HEAD_DIM, D_MODEL)[perm].reshape(BRANCH_W, D_MODEL)
        wb = jnp.stack([w_branch[l, 0], w_branch[l, 1], wb2]).astype(BF16)
        wo = w_out[l].astype(BF16)
        lvecs = [v[l].reshape(1, HEAD_DIM).astype(F32) for v in (lambda_q1, lambda_k1, lambda_q2, lambda_k2)]
        g_sub = subln_gain[l].reshape(1, LANES).astype(F32)
        sink = sink_logits[l].astype(F32)[perm] * LOG2E
        nat_bias = _nat_bias_t(natten_rpb[l])

        h_r = _ffn(h_r, norm_ffn1[l], w1i, w1o)
        h_m = _ffn(h_m, norm_ffn1[l], w1i, w1o)

        p_r, vta_r, vtb_r, vtc_r = _inproj(h_r, norm_mix[l], wi, wvt)
        p_m = _inproj(h_m, norm_mix[l], wi)
        vta_m = _meta_vt(p_m, 1024, VT_GROUPS[0], nb)
        vtb_m = _meta_vt(p_m, 2560, VT_GROUPS[1], nb)
        vtc_m = _meta_vt(p_m, 3712, VT_GROUPS[2], nb)
        ya_r = _diff_real(p_r, vta_r, p_m, vta_m, lvecs, g_sub.reshape(LANES, 1), t5["a_tall"],
                          t5["a_biasm_t"], lam_init, nb)
        yb_r = _nat_real(p_r, vtb_r, p_m, vtb_m, nat_bias, nb)
        yc_r = _gqa_real(p_r, vtc_r, p_m, vtc_m, sink, t5["c_band_t"], t5["c_biasm_t"], nb)
        ya_m, yb_m, yc_m = _meta_queries(p_r, p_m, sink, lvecs, g_sub, t5["a_mq_real"], t5["a_mq_meta"],
                                         t5["c_mq_real"], t5["c_mq_meta"], lam_init, nb)
        h_r = _merge(h_r, norm_mix[l], ya_r, yb_r, yc_r, wg, wb, wo)
        h_m = _merge(h_m, norm_mix[l], ya_m, yb_m, yc_m, wg, wb, wo)

        last = l == depth - 1
        h_r = _ffn(h_r, norm_ffn2[l], w2i, w2o, final_gain=final_norm if last else None)
        h_m = _ffn(h_m, norm_ffn2[l], w2i, w2o)
        out = h_r
    return out.reshape(nb, SEQ, D_MODEL)
```

```python
import functools
import math

import numpy as np
import jax
import jax.numpy as jnp
from jax import lax
from jax.experimental import pallas as pl
from jax.experimental.pallas import tpu as pltpu

F32 = jnp.float32
BF16 = jnp.bfloat16

D_MODEL = 1024
SEQ = 4096
N_META = 16
GRID_W = 64
HEAD_DIM = 64
LANES = 128
A_HEADS = 4
B_HEADS = 8
C_HEADS = 8
C_KV_HEADS = 2
NA_ROWS = 8
NA_COLS = 16
WINDOW = 128
T5_BUCKETS = 32
D_FF = 2816
BRANCH_W = 512
IN_COLS = 3840
EPS = 1e-6
NEG = -1e30
SCALE = HEAD_DIM ** -0.5
LOG2E = math.log2(math.e)

QA_BLK, KA_BLK, VA_BLK, QB_BLK, KB_BLK, VB_BLK, QC_BLK = 0, 1, 2, 3, 4, 5, 6
KC_BLK128, VC_BLK128 = 28, 29
C_PERM = (0, 4, 1, 5, 2, 6, 3, 7)

TM_REAL = 1024
FF_CHUNK = 256
PROJ_CHUNK = 768
TQ = 256
TK = 256
TKD = 256
LOOKAHEAD = 2
VT_ROWS = 144
VT_GROUPS = (A_HEADS, B_HEADS // 2, 1)
VT_CHUNK = (TKD, 256, 128)
BIAS_TALL = 6 * TQ + TKD - TQ
NAT_R = 4
NAT_W = 12
GQA_KW = 512
VMEM_LIMIT = 56 * 1024 * 1024


def _cparams(n_axes, **kw):
    return pltpu.CompilerParams(dimension_semantics=("parallel",) * n_axes,
                                vmem_limit_bytes=VMEM_LIMIT, **kw)


def _const_spec(shape, n_grid):
    zeros = (0,) * len(shape)
    if n_grid == 1:
        return pl.BlockSpec(shape, lambda i: zeros, pipeline_mode=pl.Buffered(1))
    return pl.BlockSpec(shape, lambda i, j: zeros, pipeline_mode=pl.Buffered(1))


def _rms(x, g):
    return x * lax.rsqrt(jnp.mean(x * x, axis=-1, keepdims=True) + EPS) * g


def _dot(a, b):
    return jnp.dot(a, b, preferred_element_type=F32)


def _dot_t(a, b):
    return lax.dot_general(a, b, (((1,), (1,)), ((), ())), preferred_element_type=F32)


def _ffn_kernel(h_ref, g_ref, win_ref, wout_ref, *rest, final):
    if final:
        fg_ref, o_ref, act_ref = rest
    else:
        o_ref, act_ref = rest
    x = h_ref[...]
    xn = _rms(x, g_ref[...]).astype(BF16)
    for c in range(D_FF // FF_CHUNK):
        lo = c * FF_CHUNK
        gg = _dot(xn, win_ref[:, lo:lo + FF_CHUNK])
        uu = _dot(xn, win_ref[:, D_FF + lo:D_FF + lo + FF_CHUNK])
        act_ref[:, lo:lo + FF_CHUNK] = (gg * jax.nn.sigmoid(gg) * uu).astype(BF16)
    hn = x + 0.5 * _dot(act_ref[...], wout_ref[...])
    if final:
        hn = _rms(hn, fg_ref[...])
    o_ref[...] = hn


def _ffn(h, gain, w_in, w_out, final_gain=None):
    rows = h.shape[0]
    tm = min(TM_REAL, rows)
    final = final_gain is not None
    in_specs = [
        pl.BlockSpec((tm, D_MODEL), lambda i: (i, 0)),
        _const_spec((1, D_MODEL), 1),
        _const_spec((D_MODEL, 2 * D_FF), 1),
        _const_spec((D_FF, D_MODEL), 1),
    ]
    args = [h, gain.reshape(1, D_MODEL), w_in, w_out]
    if final:
        in_specs.append(_const_spec((1, D_MODEL), 1))
        args.append(final_gain.reshape(1, D_MODEL))
    return pl.pallas_call(
        functools.partial(_ffn_kernel, final=final),
        grid=(rows // tm,),
        in_specs=in_specs,
        out_specs=pl.BlockSpec((tm, D_MODEL), lambda i: (i, 0)),
        out_shape=jax.ShapeDtypeStruct((rows, D_MODEL), F32),
        scratch_shapes=[pltpu.VMEM((tm, D_FF), BF16)],
        compiler_params=_cparams(1),
        name="ffn",
    )(*args)


def _inproj_kernel(h_ref, g_ref, w_ref, *rest, transposed):
    if transposed:
        wvt_ref, o_ref, *vt_refs = rest
    else:
        (o_ref,) = rest
    xn = _rms(h_ref[...], g_ref[...]).astype(BF16)
    for c in range(IN_COLS // PROJ_CHUNK):
        lo = c * PROJ_CHUNK
        o_ref[:, lo:lo + PROJ_CHUNK] = _dot(xn, w_ref[:, lo:lo + PROJ_CHUNK]).astype(BF16)
    if transposed:
        tm = xn.shape[0]
        vt = _dot_t(wvt_ref[...], xn).astype(BF16)
        base = 0
        for vt_ref, groups, chunk in zip(vt_refs, VT_GROUPS, VT_CHUNK):
            for c in range(tm // chunk):
                cols = slice(c * chunk, (c + 1) * chunk)
                for g in range(groups):
                    vt_ref[c, g * VT_ROWS:g * VT_ROWS + LANES, :] = vt[base + g * LANES:base + (g + 1) * LANES, cols]
                    vt_ref[c, g * VT_ROWS + LANES:(g + 1) * VT_ROWS, :] = jnp.ones((VT_ROWS - LANES, chunk), BF16)
            base += groups * LANES


def _inproj(h, gain, w_in, wvt=None):
    rows = h.shape[0]
    tm = min(TM_REAL, rows)
    transposed = wvt is not None
    in_specs = [
        pl.BlockSpec((tm, D_MODEL), lambda i: (i, 0)),
        _const_spec((1, D_MODEL), 1),
        _const_spec((D_MODEL, IN_COLS), 1),
    ]
    args = [h, gain.reshape(1, D_MODEL), w_in]
    out_specs = [pl.BlockSpec((tm, IN_COLS), lambda i: (i, 0))]
    out_shape = [jax.ShapeDtypeStruct((rows, IN_COLS), BF16)]
    if transposed:
        in_specs.append(_const_spec(wvt.shape, 1))
        args.append(wvt)
        for groups, chunk in zip(VT_GROUPS, VT_CHUNK):
            out_specs.append(pl.BlockSpec((tm // chunk, groups * VT_ROWS, chunk), lambda i: (i, 0, 0)))
            out_shape.append(jax.ShapeDtypeStruct((rows // chunk, groups * VT_ROWS, chunk), BF16))
    res = pl.pallas_call(
        functools.partial(_inproj_kernel, transposed=transposed),
        grid=(rows // tm,),
        in_specs=in_specs,
        out_specs=out_specs,
        out_shape=out_shape,
        compiler_params=_cparams(1),
        name="inproj",
    )(*args)
    return res if transposed else res[0]


def _merge_kernel(h_ref, g_ref, ya_ref, yb_ref, yc_ref, wg_ref, wb_ref, wo_ref, o_ref):
    x = h_ref[...]
    xn = _rms(x, g_ref[...]).astype(BF16)
    merged = None
    for i, y_ref in enumerate((ya_ref, yb_ref, yc_ref)):
        term = jax.nn.sigmoid(_dot(xn, wg_ref[i])) * _dot(y_ref[...], wb_ref[i])
        merged = term if merged is None else merged + term
    o_ref[...] = x + _dot(merged.astype(BF16), wo_ref[...])


def _merge(h, gain, ya, yb, yc, w_gate, w_branch, w_out):
    rows = h.shape[0]
    tm = min(TM_REAL, rows)
    row_spec = lambda w: pl.BlockSpec((tm, w), lambda i: (i, 0))
    return pl.pallas_call(
        _merge_kernel,
        grid=(rows // tm,),
        in_specs=[
            row_spec(D_MODEL),
            _const_spec((1, D_MODEL), 1),
            row_spec(BRANCH_W), row_spec(BRANCH_W), row_spec(BRANCH_W),
            _const_spec((3, D_MODEL, D_MODEL), 1),
            _const_spec((3, BRANCH_W, D_MODEL), 1),
            _const_spec((D_MODEL, D_MODEL), 1),
        ],
        out_specs=row_spec(D_MODEL),
        out_shape=jax.ShapeDtypeStruct((rows, D_MODEL), F32),
        compiler_params=_cparams(1),
        name="merge",
    )(h, gain.reshape(1, D_MODEL), ya, yb, yc, w_gate, w_branch, w_out)


def _lane_lo(rows):
    return lax.broadcasted_iota(jnp.int32, (rows, LANES), 1) < HEAD_DIM


def _split_halves(q):
    lo = _lane_lo(q.shape[0])
    zero = jnp.zeros_like(q)
    return jnp.concatenate([jnp.where(lo, q, zero), jnp.where(lo, zero, q)], axis=0)


def _pad_rows(x, rows):
    return jnp.concatenate([x, jnp.zeros((rows - x.shape[0], x.shape[1]), x.dtype)], axis=0)


def _twice(b):
    return jnp.concatenate([b, b], axis=0)


def _side_by_side(a, b):
    return jnp.concatenate([a, b], axis=1)


def _lam(lq1_ref, lk1_ref, lq2_ref, lk2_ref, lam_init):
    s1 = jnp.sum(lq1_ref[...] * lk1_ref[...], axis=-1, keepdims=True)
    s2 = jnp.sum(lq2_ref[...] * lk2_ref[...], axis=-1, keepdims=True)
    return jnp.exp(s1) - jnp.exp(s2) + lam_init


def _diff_finish(o, lam, g, lam_init):
    m = o.shape[0] // 2
    d = o[:m] - lam * o[m:]
    return _rms(d, g) * (1.0 - lam_init)


def _pipeline(n, issue, consume, scr):
    pending = {}
    for t in range(min(LOOKAHEAD, n)):
        pending[t] = issue(t, scr[t % len(scr)])
    for t in range(n):
        if t + LOOKAHEAD < n:
            pending[t + LOOKAHEAD] = issue(t + LOOKAHEAD, scr[(t + LOOKAHEAD) % len(scr)])
        consume(t, scr[t % len(scr)], pending.pop(t))


def _pair_out(o):
    top = lax.broadcasted_iota(jnp.int32, (LANES, TQ), 0) < HEAD_DIM
    return jnp.where(top, o[:, :TQ], o[:, TQ:]).T


def _diff_real_kernel(far_ref, lq1_ref, lk1_ref, lq2_ref, lk2_ref, g_ref, q_ref, k_ref, vt_ref, km_ref, vmt_ref,
                      bias_ref, biasm_ref, o_ref, *scr, lam_init):
    i = pl.program_id(1)
    lam = _lam(lq1_ref, lk1_ref, lq2_ref, lk2_ref, lam_init)
    g = g_ref[...]
    nk = k_ref.shape[0] // TKD

    def head_setup(h):
        hs = slice(h * LANES, (h + 1) * LANES)
        q12 = _split_halves(q_ref[:, hs])
        s = _dot_t(km_ref[:, hs], q12) + _side_by_side(biasm_ref[h], biasm_ref[h])
        m0 = jnp.max(s, axis=0, keepdims=True)
        p = jnp.exp2(s - m0).astype(BF16)
        acc0 = _dot(vmt_ref[0, h * VT_ROWS:(h + 1) * VT_ROWS, :], p)
        return q12, m0, acc0

    ratio = TKD // TQ
    jc = (i + ratio - 1) // ratio
    near_slots = 2 if ratio > 1 else 3

    def scores(h, u, q12, dst_ref):
        ju = jc - 1 + u
        j = jnp.where(ju >= nk, ju - nk, jnp.where(ju < 0, ju + nk, ju))
        kt = k_ref[pl.ds(pl.multiple_of(j * TKD, TKD), TKD), h * LANES:(h + 1) * LANES]
        s = _dot_t(kt, q12)
        if u < near_slots:
            e = jnp.clip(ratio * j - i, -3, 2) + 3
            bt = bias_ref[h, pl.ds(pl.multiple_of(e * TQ, TQ), TKD), :]
            s = s + _side_by_side(bt, bt)
            c = None
        else:
            c = jnp.where(ju >= nk, far_ref[0, h], far_ref[1, h])
        dst_ref[...] = s
        smax = jnp.max(s, axis=0, keepdims=True)
        return j, c, smax if c is None else smax + c

    def fold(h, src_ref, issued, m, acc):
        j, c, smax = issued
        m_new = jnp.maximum(m, smax)
        alpha = jnp.exp2(m - m_new)
        p = jnp.exp2(src_ref[...] - (m_new if c is None else m_new - c)).astype(BF16)
        return m_new, alpha * acc + _dot(vt_ref[j, h * VT_ROWS:(h + 1) * VT_ROWS, :], p)

    def finish(h, acc):
        o = acc[:LANES] / acc[LANES:LANES + 1]
        d = o[:, :TQ] - lam * o[:, TQ:]
        y = d * lax.rsqrt(jnp.mean(d * d, axis=0, keepdims=True) + EPS) * g * (1.0 - lam_init)
        o_ref[:, h * LANES:(h + 1) * LANES] = y.T.astype(BF16)

    tiles = [(h, j) for h in range(A_HEADS) for j in range(nk)]
    setup = {}
    smax = {}

    def issue(t):
        h, j = tiles[t]
        if h not in setup:
            setup[h] = head_setup(h)
        smax[t] = scores(h, j, setup[h][0], scr[t % len(scr)])

    for t in range(LOOKAHEAD):
        issue(t)
    m = acc = None
    for t, (h, j) in enumerate(tiles):
        if t + LOOKAHEAD < len(tiles):
            issue(t + LOOKAHEAD)
        if j == 0:
            _, m, acc = setup[h]
        m, acc = fold(h, scr[t % len(scr)], smax.pop(t), m, acc)
        if j == nk - 1:
            finish(h, acc)


def _diff_real(proj_r, vt, proj_m, vmt, lvecs, g_col, bias_far, bias_tall, biasm_t, lam_init, nb):
    nq = SEQ // TQ
    lspec = _const_spec((1, HEAD_DIM), 2)
    return pl.pallas_call(
        functools.partial(_diff_real_kernel, lam_init=lam_init),
        grid=(nb, nq),
        in_specs=[
            pl.BlockSpec(memory_space=pltpu.SMEM),
            lspec, lspec, lspec, lspec,
            _const_spec((LANES, 1), 2),
            pl.BlockSpec((TQ, 512), lambda b, i: (b * nq + i, QA_BLK)),
            pl.BlockSpec((SEQ, 512), lambda b, i: (b, KA_BLK)),
            pl.BlockSpec((SEQ // TKD, A_HEADS * VT_ROWS, TKD), lambda b, i: (b, 0, 0)),
            pl.BlockSpec((N_META, 512), lambda b, i: (b, KA_BLK)),
            pl.BlockSpec((1, A_HEADS * VT_ROWS, N_META), lambda b, i: (b, 0, 0)),
            _const_spec((A_HEADS, BIAS_TALL, TQ), 2),
            pl.BlockSpec((A_HEADS, N_META, TQ), lambda b, i: (0, 0, i)),
        ],
        out_specs=pl.BlockSpec((TQ, 512), lambda b, i: (b * nq + i, 0)),
        out_shape=jax.ShapeDtypeStruct((nb * SEQ, BRANCH_W), BF16),
        scratch_shapes=[pltpu.VMEM((TKD, 2 * TQ), F32)] * (LOOKAHEAD + 1),
        compiler_params=_cparams(2),
        name="diff_real",
    )(bias_far, *lvecs, g_col, proj_r, proj_r, vt, proj_m, vmt, bias_tall, biasm_t)


def _nat_real_kernel(q_ref, k_ref, vt_ref, km_ref, vmt_ref, bias_ref, o_ref, *scr):
    blk = pl.program_id(1)
    chunk = vt_ref.shape[2]
    n_chunks = NAT_W * GRID_W // chunk
    c0 = jnp.clip(blk - 1, 0, vt_ref.shape[0] - n_chunks)
    start = pl.multiple_of(c0 * chunk, chunk)

    def scores(pr, dst_ref):
        hs = slice(pr * LANES, (pr + 1) * LANES)
        q2 = _split_halves(q_ref[:, hs])
        kw = k_ref[pl.ds(start, n_chunks * chunk), hs]
        s = _dot_t(kw, q2) + _side_by_side(bias_ref[0, 2 * pr], bias_ref[0, 2 * pr + 1])
        dst_ref[...] = s
        sm = _dot_t(km_ref[:, hs], q2)
        return sm, jnp.maximum(jnp.max(s, axis=0, keepdims=True), jnp.max(sm, axis=0, keepdims=True))

    def finish(pr, src_ref, pending):
        sm, m = pending
        rows = slice(pr * VT_ROWS, (pr + 1) * VT_ROWS)
        acc = _dot(vmt_ref[0, rows, :], jnp.exp2(sm - m).astype(BF16))
        for c in range(n_chunks):
            p = jnp.exp2(src_ref[c * chunk:(c + 1) * chunk, :] - m).astype(BF16)
            acc = acc + _dot(vt_ref[c0 + c, rows, :], p)
        o_ref[:, pr * LANES:(pr + 1) * LANES] = _pair_out(acc[:LANES] / acc[LANES:LANES + 1]).astype(BF16)

    _pipeline(B_HEADS // 2, scores, finish, scr)


def _nat_real(proj_r, vt, proj_m, vmt, bias_t, nb):
    nq = SEQ // TQ
    nkeys = NAT_W * GRID_W
    chunk = VT_CHUNK[1]

    def variant(b, i):
        return (jnp.where(i == 0, 0, jnp.where(i == nq - 1, 2, 1)), 0, 0, 0)

    return pl.pallas_call(
        _nat_real_kernel,
        grid=(nb, nq),
        in_specs=[
            pl.BlockSpec((TQ, 512), lambda b, i: (b * nq + i, QB_BLK)),
            pl.BlockSpec((SEQ, 512), lambda b, i: (b, KB_BLK)),
            pl.BlockSpec((SEQ // chunk, VT_GROUPS[1] * VT_ROWS, chunk), lambda b, i: (b, 0, 0)),
            pl.BlockSpec((N_META, 512), lambda b, i: (b, KB_BLK)),
            pl.BlockSpec((1, VT_GROUPS[1] * VT_ROWS, N_META), lambda b, i: (b, 0, 0)),
            pl.BlockSpec((1, B_HEADS, nkeys, TQ), variant),
        ],
        out_specs=pl.BlockSpec((TQ, 512), lambda b, i: (b * nq + i, 0)),
        out_shape=jax.ShapeDtypeStruct((nb * SEQ, BRANCH_W), BF16),
        scratch_shapes=[pltpu.VMEM((nkeys, 2 * TQ), F32)] * (LOOKAHEAD + 1),
        compiler_params=_cparams(2),
        name="nat_real",
    )(proj_r, proj_r, vt, proj_m, vmt, bias_t)


def _gqa_real_kernel(sink_ref, q_ref, k_ref, vt_ref, km_ref, vmt_ref, band_ref, mb_ref, o_ref, *scr):
    t = pl.program_id(1)
    chunk = vt_ref.shape[2]
    n_chunks = GQA_KW // chunk
    c0 = jnp.clip((TQ // chunk) * t - WINDOW // chunk, 0, vt_ref.shape[0] - n_chunks)
    kw = k_ref[pl.ds(pl.multiple_of(c0 * chunk, chunk), GQA_KW), :]
    vts = [_side_by_side(vt_ref[c0 + 2 * c], vt_ref[c0 + 2 * c + 1]) for c in range(n_chunks // 2)]
    npair = C_HEADS // 2

    def scores(pr, dst_ref):
        q2 = _split_halves(q_ref[:, pr * LANES:(pr + 1) * LANES])
        s = _dot_t(kw, q2) + _side_by_side(band_ref[0, pr], band_ref[0, npair + pr])
        dst_ref[...] = s
        sm = _dot_t(km_ref[...], q2) + _side_by_side(mb_ref[pr], mb_ref[npair + pr])
        sink = _side_by_side(jnp.full((1, TQ), sink_ref[pr], F32), jnp.full((1, TQ), sink_ref[npair + pr], F32))
        m = jnp.maximum(jnp.maximum(jnp.max(s, axis=0, keepdims=True), jnp.max(sm, axis=0, keepdims=True)), sink)
        return sm, sink, m

    def finish(pr, src_ref, pending):
        sm, sink, m = pending
        acc = _dot(vmt_ref[0], jnp.exp2(sm - m).astype(BF16))
        for c, vt2 in enumerate(vts):
            p = jnp.exp2(src_ref[2 * c * chunk:2 * (c + 1) * chunk, :] - m).astype(BF16)
            acc = acc + _dot(vt2, p)
        l = acc[LANES:LANES + 1] + jnp.exp2(sink - m)
        o_ref[:, pr * LANES:(pr + 1) * LANES] = _pair_out(acc[:LANES] / l).astype(BF16)

    _pipeline(npair, scores, finish, scr)


def _gqa_real(proj_r, vt, proj_m, vmt, sink, band_t, mbias_t, nb):
    nq = SEQ // TQ
    chunk = VT_CHUNK[2]

    def variant(b, i):
        return (jnp.where(i == 0, 0, jnp.where(i == nq - 1, 2, 1)), 0, 0, 0)

    return pl.pallas_call(
        _gqa_real_kernel,
        grid=(nb, nq),
        in_specs=[
            pl.BlockSpec(memory_space=pltpu.SMEM),
            pl.BlockSpec((TQ, 512), lambda b, i: (b * nq + i, QC_BLK)),
            pl.BlockSpec((SEQ, LANES), lambda b, i: (b, KC_BLK128)),
            pl.BlockSpec((SEQ // chunk, VT_ROWS, chunk), lambda b, i: (b, 0, 0)),
            pl.BlockSpec((N_META, LANES), lambda b, i: (b, KC_BLK128)),
            pl.BlockSpec((1, VT_ROWS, N_META), lambda b, i: (b, 0, 0)),
            pl.BlockSpec((1, C_HEADS, GQA_KW, TQ), variant),
            pl.BlockSpec((C_HEADS, N_META, TQ), lambda b, i: (0, 0, i)),
        ],
        out_specs=pl.BlockSpec((TQ, 512), lambda b, i: (b * nq + i, 0)),
        out_shape=jax.ShapeDtypeStruct((nb * SEQ, BRANCH_W), BF16),
        scratch_shapes=[pltpu.VMEM((GQA_KW, 2 * TQ), F32)] * (LOOKAHEAD + 1),
        compiler_params=_cparams(2),
        name="gqa_real",
    )(sink, proj_r, proj_r, vt, proj_m, vmt, band_t, mbias_t)


def _meta_kernel(sink_ref, lq1_ref, lk1_ref, lq2_ref, lk2_ref, g_ref, pm_ref, ka_ref, va_ref, kb_ref, vb_ref,
                 kc_ref, vc_ref, abr_ref, abm_ref, cbr_ref, cbm_ref, oa_ref, ob_ref, oc_ref, *, lam_init):
    lam = _lam(lq1_ref, lk1_ref, lq2_ref, lk2_ref, lam_init)
    g = g_ref[...]
    lo = _lane_lo(N_META)
    lane = lax.broadcasted_iota(jnp.int32, (1, LANES), 1)
    pad_bias = jnp.where(lane < N_META, 0.0, NEG)

    def col(blk512, sub):
        base = blk512 * 512 + sub * LANES
        return slice(base, base + LANES)

    for h in range(A_HEADS):
        hs = slice(h * LANES, (h + 1) * LANES)
        q12 = _split_halves(pm_ref[:, col(QA_BLK, h)])
        km = _pad_rows(pm_ref[:, col(KA_BLK, h)], LANES)
        vm = _pad_rows(pm_ref[:, col(VA_BLK, h)], LANES)
        s = _dot_t(q12, ka_ref[:, hs]) + _twice(abr_ref[h])
        sm = _dot_t(q12, km) + _twice(abm_ref[h])
        m = jnp.maximum(jnp.max(s, axis=-1, keepdims=True), jnp.max(sm, axis=-1, keepdims=True))
        e = jnp.exp2(s - m)
        em = jnp.exp2(sm - m)
        l = jnp.sum(e, axis=-1, keepdims=True) + jnp.sum(em, axis=-1, keepdims=True)
        o = (_dot(e.astype(BF16), va_ref[:, hs]) + _dot(em.astype(BF16), vm)) / l
        oa_ref[:, hs] = _diff_finish(o, lam, g, lam_init).astype(BF16)

    org = jnp.bitwise_and(lax.broadcasted_iota(jnp.int32, (1, NA_ROWS * GRID_W), 1), GRID_W - 1) < NA_COLS
    org_bias = jnp.where(org, 0.0, NEG)
    for pr in range(B_HEADS // 2):
        hs = slice(pr * LANES, (pr + 1) * LANES)
        q2 = _split_halves(pm_ref[:, col(QB_BLK, pr)])
        km = _pad_rows(pm_ref[:, col(KB_BLK, pr)], LANES)
        vm = _pad_rows(pm_ref[:, col(VB_BLK, pr)], LANES)
        s = _dot_t(q2, kb_ref[:, hs]) + org_bias
        sm = _dot_t(q2, km) + pad_bias
        m = jnp.maximum(jnp.max(s, axis=-1, keepdims=True), jnp.max(sm, axis=-1, keepdims=True))
        e = jnp.exp2(s - m)
        em = jnp.exp2(sm - m)
        l = jnp.sum(e, axis=-1, keepdims=True) + jnp.sum(em, axis=-1, keepdims=True)
        o = (_dot(e.astype(BF16), vb_ref[:, hs]) + _dot(em.astype(BF16), vm)) / l
        ob_ref[:, hs] = jnp.where(lo, o[:N_META], o[N_META:]).astype(BF16)

    kc_base = KC_BLK128 * LANES
    vc_base = VC_BLK128 * LANES
    km = _pad_rows(pm_ref[:, kc_base:kc_base + LANES], LANES)
    vm = _pad_rows(pm_ref[:, vc_base:vc_base + LANES], LANES)
    npair = C_HEADS // 2
    for pr in range(npair):
        hs = slice(pr * LANES, (pr + 1) * LANES)
        q2 = _split_halves(pm_ref[:, col(QC_BLK, pr)])
        s = _dot_t(q2, kc_ref[...]) + jnp.concatenate([cbr_ref[pr], cbr_ref[npair + pr]], axis=0)
        sm = _dot_t(q2, km) + jnp.concatenate([cbm_ref[pr], cbm_ref[npair + pr]], axis=0)
        sink = jnp.concatenate([jnp.full((N_META, 1), sink_ref[pr], F32),
                                jnp.full((N_META, 1), sink_ref[npair + pr], F32)], axis=0)
        m = jnp.maximum(jnp.maximum(jnp.max(s, axis=-1, keepdims=True), jnp.max(sm, axis=-1, keepdims=True)), sink)
        e = jnp.exp2(s - m)
        em = jnp.exp2(sm - m)
        l = jnp.sum(e, axis=-1, keepdims=True) + jnp.sum(em, axis=-1, keepdims=True) + jnp.exp2(sink - m)
        o = (_dot(e.astype(BF16), vc_ref[...]) + _dot(em.astype(BF16), vm)) / l
        oc_ref[:, hs] = jnp.where(lo, o[:N_META], o[N_META:]).astype(BF16)


def _meta_queries(proj_r, proj_m, sink, lvecs, g_sub, abr, abm, cbr, cbm, lam_init, nb):
    lspec = _const_spec((1, HEAD_DIM), 1)
    org_rows = NA_ROWS * GRID_W
    out = jax.ShapeDtypeStruct((nb * N_META, BRANCH_W), BF16)
    out_spec = pl.BlockSpec((N_META, BRANCH_W), lambda b: (b, 0))
    return pl.pallas_call(
        functools.partial(_meta_kernel, lam_init=lam_init),
        grid=(nb,),
        in_specs=[
            pl.BlockSpec(memory_space=pltpu.SMEM),
            lspec, lspec, lspec, lspec,
            _const_spec((1, LANES), 1),
            pl.BlockSpec((N_META, IN_COLS), lambda b: (b, 0)),
            pl.BlockSpec((SEQ, 512), lambda b: (b, KA_BLK)),
            pl.BlockSpec((SEQ, 512), lambda b: (b, VA_BLK)),
            pl.BlockSpec((org_rows, 512), lambda b: (b * (SEQ // org_rows), KB_BLK)),
            pl.BlockSpec((org_rows, 512), lambda b: (b * (SEQ // org_rows), VB_BLK)),
            pl.BlockSpec((WINDOW, LANES), lambda b: (b * (SEQ // WINDOW), KC_BLK128)),
            pl.BlockSpec((WINDOW, LANES), lambda b: (b * (SEQ // WINDOW), VC_BLK128)),
            _const_spec((A_HEADS, N_META, SEQ), 1),
            _const_spec((A_HEADS, N_META, LANES), 1),
            _const_spec((C_HEADS, N_META, WINDOW), 1),
            _const_spec((C_HEADS, N_META, LANES), 1),
        ],
        out_specs=[out_spec, out_spec, out_spec],
        out_shape=[out, out, out],
        compiler_params=_cparams(1),
        name="meta_queries",
    )(sink, *lvecs, g_sub, proj_m, proj_r, proj_r, proj_r, proj_r, proj_r, proj_r, abr, abm, cbr, cbm)


def _t5_bucket_np(rel):
    nb = T5_BUCKETS // 2
    max_exact = nb // 2
    rel = np.asarray(rel, np.int64)
    n = np.abs(rel)
    n2 = np.maximum(n, 1) ** 2
    large = np.minimum(np.floor(np.log2(n2.astype(np.float64))).astype(np.int64) + 2, nb - 1)
    return (np.where(rel > 0, nb, 0) + np.where(n < max_exact, n, large)).astype(np.int32)


def _t5_vals(table, rel):
    idx = _t5_bucket_np(rel)
    v = jnp.take(table.astype(F32), jnp.asarray(idx.reshape(-1)), axis=0)
    return jnp.moveaxis(v, -1, 0).reshape((table.shape[1],) + idx.shape)


def _toeplitz(w, n, m):
    p = n + m - 1
    wp = jnp.concatenate([w, jnp.zeros(w.shape[:-1] + (1,), w.dtype)], axis=-1)
    flat = jnp.tile(wp, (1,) * (w.ndim - 1) + (n,))[..., :n * p]
    return flat.reshape(w.shape[:-1] + (n, p))[..., n - 1:n - 1 + m]


def _pad_lanes_neg(x):
    pad = jnp.full(x.shape[:-1] + (LANES - x.shape[-1],), NEG, F32)
    return jnp.concatenate([x, pad], axis=-1)


def _t5_tables(t5_table):
    table = t5_table.astype(F32) * LOG2E
    ta = table[:, :A_HEADS]
    tc = table[:, A_HEADS:]
    out = {}
    rel = (BIAS_TALL - 1 - 3 * TQ) - np.arange(BIAS_TALL + TQ - 1)
    out["a_tall"] = _toeplitz(_t5_vals(ta, rel), BIAS_TALL, TQ)
    out["a_far"] = _t5_vals(ta, np.array([-(SEQ + N_META), SEQ + N_META])).T
    rel_t = np.arange(N_META)[:, None] - (N_META + np.arange(TQ))[None, :]
    far_t = np.full((1, 1), -(SEQ + N_META))

    def meta_key_bias_t(tab):
        rest = jnp.broadcast_to(_t5_vals(tab, far_t), (tab.shape[1], N_META, SEQ - TQ))
        return jnp.concatenate([_t5_vals(tab, rel_t), rest], axis=-1)

    out["a_biasm_t"] = meta_key_bias_t(ta)
    out["c_biasm_t"] = meta_key_bias_t(tc)
    offs = np.array([0, -WINDOW, -(GQA_KW - TQ)])
    relw = (GQA_KW - 1) - np.arange(GQA_KW + TQ - 1)[None, :] + offs[:, None]
    band = _toeplitz(_t5_vals(tc, relw), GQA_KW, TQ)
    relb = (np.arange(GQA_KW)[None, :, None] - np.arange(TQ)[None, None, :] + offs[:, None, None])
    band = jnp.where(jnp.asarray(np.abs(relb) <= WINDOW)[None], band, NEG)
    out["c_band_t"] = jnp.swapaxes(band, 0, 1)
    mq = np.arange(N_META)[:, None]
    near_k = N_META + np.arange(TK)[None, :] - mq
    far_k = np.full((1, 1), SEQ + N_META)
    first = _t5_vals(ta, near_k)
    rest = jnp.broadcast_to(_t5_vals(ta, far_k), (A_HEADS, N_META, SEQ - TK))
    out["a_mq_real"] = jnp.concatenate([first, rest], axis=-1)
    out["a_mq_meta"] = _pad_lanes_neg(_t5_vals(ta, np.arange(N_META)[None, :] - mq))
    rel0 = N_META + np.arange(WINDOW)[None, :] - mq
    out["c_mq_real"] = jnp.where(jnp.asarray(rel0 <= WINDOW)[None], _t5_vals(tc, rel0), NEG)
    out["c_mq_meta"] = _pad_lanes_neg(_t5_vals(tc, np.arange(N_META)[None, :] - mq))
    return out


def _nat_bias_t(rpb):
    n_rows = SEQ // GRID_W
    w = jnp.pad(rpb.astype(F32) * LOG2E, ((0, 0), (0, 0), (GRID_W - NA_COLS, GRID_W - NA_COLS)))
    tiles = _toeplitz(w, GRID_W, GRID_W)
    cols = np.arange(GRID_W)
    cstart = np.clip(cols - NA_COLS // 2, 0, GRID_W - NA_COLS)
    col_ok = (cols[None, :] >= cstart[:, None]) & (cols[None, :] < cstart[:, None] + NA_COLS)
    tiles = jnp.where(jnp.asarray(col_ok)[None, None], tiles, NEG)
    neg_tile = jnp.full((B_HEADS, 1, GRID_W, GRID_W), NEG, F32)
    tiles = jnp.concatenate([tiles, neg_tile], axis=1)
    n_blk = n_rows // NAT_R
    idx = np.zeros((3, NAT_R, NAT_W), np.int32)
    for v, blk in enumerate((0, 1, n_blk - 1)):
        r0 = NAT_R * blk
        w0 = int(np.clip(r0 - NA_ROWS // 2, 0, n_rows - NAT_W))
        for qr in range(NAT_R):
            r = r0 + qr
            rs = int(np.clip(r - NA_ROWS // 2, 0, n_rows - NA_ROWS))
            for kr in range(NAT_W):
                krow = w0 + kr
                idx[v, qr, kr] = krow - r + NA_ROWS - 1 if rs <= krow < rs + NA_ROWS else 2 * NA_ROWS - 1
    g = jnp.take(tiles, jnp.asarray(idx.reshape(-1)), axis=1)
    g = g.reshape(B_HEADS, 3, NAT_R, NAT_W, GRID_W, GRID_W)
    g = jnp.transpose(g, (1, 0, 3, 5, 2, 4))
    return g.reshape(3, B_HEADS, NAT_W * GRID_W, NAT_R * GRID_W)


def _prep_w_in(w):
    col = np.arange(IN_COLS)
    is_q = (col < 512) | ((col >= 1536) & (col < 2048)) | ((col >= 3072) & (col < 3584))
    w = w * jnp.asarray(np.where(is_q, SCALE * LOG2E, 1.0), F32)[None, :]
    qc = w[:, 3072:3584].reshape(D_MODEL, C_HEADS, HEAD_DIM)[:, np.asarray(C_PERM)].reshape(D_MODEL, 512)
    w = jnp.concatenate([w[:, :3072], qc, w[:, 3584:]], axis=1)
    return w.astype(BF16)


def _meta_vt(p_m, lo, groups, nb):
    v = p_m[:, lo:lo + groups * LANES].reshape(nb, N_META, groups, LANES)
    v = jnp.transpose(v, (0, 2, 3, 1))
    ones = jnp.ones((nb, groups, VT_ROWS - LANES, N_META), BF16)
    return jnp.concatenate([v, ones], axis=2).reshape(nb, groups * VT_ROWS, N_META)


def kernel(x, meta_tokens, t5_table, norm_ffn1, w_ffn1_in, w_ffn1_out, norm_mix, w_in, lambda_q1, lambda_k1,
           lambda_q2, lambda_k2, subln_gain, natten_rpb, sink_logits, w_branch, w_gate, w_out, norm_ffn2,
           w_ffn2_in, w_ffn2_out, final_norm):
    nb, seq, d = x.shape
    assert (seq, d) == (SEQ, D_MODEL)
    depth = norm_ffn1.shape[0]
    h_r = x.reshape(nb * SEQ, D_MODEL)
    h_m = jnp.broadcast_to(meta_tokens[None].astype(x.dtype), (nb, N_META, D_MODEL)).reshape(nb * N_META, D_MODEL)
    t5 = _t5_tables(t5_table)
    perm = np.asarray(C_PERM)
    out = None
    for l in range(depth):
        lam_init = 0.8 - 0.6 * math.exp(-0.3 * l)
        w1i, w1o = w_ffn1_in[l].astype(BF16), w_ffn1_out[l].astype(BF16)
        w2i, w2o = w_ffn2_in[l].astype(BF16), w_ffn2_out[l].astype(BF16)
        wi = _prep_w_in(w_in[l])
        wvt = jnp.concatenate([wi[:, 1024:1536], wi[:, 2560:3072], wi[:, 3712:3840]], axis=1).T
        wg = w_gate[l].astype(BF16)
        wb2 = w_branch[l, 2].reshape(C_HEADS, HEAD_DIM, D_MODEL)[perm].reshape(BRANCH_W, D_MODEL)
        wb = jnp.stack([w_branch[l, 0], w_branch[l, 1], wb2]).astype(BF16)
        wo = w_out[l].astype(BF16)
        lvecs = [v[l].reshape(1, HEAD_DIM).astype(F32) for v in (lambda_q1, lambda_k1, lambda_q2, lambda_k2)]
        g_sub = subln_gain[l].reshape(1, LANES).astype(F32)
        sink = sink_logits[l].astype(F32)[perm] * LOG2E
        nat_bias = _nat_bias_t(natten_rpb[l])

        h_r = _ffn(h_r, norm_ffn1[l], w1i, w1o)
        h_m = _ffn(h_m, norm_ffn1[l], w1i, w1o)

        p_r, vta_r, vtb_r, vtc_r = _inproj(h_r, norm_mix[l], wi, wvt)
        p_m = _inproj(h_m, norm_mix[l], wi)
        vta_m = _meta_vt(p_m, 1024, VT_GROUPS[0], nb)
        vtb_m = _meta_vt(p_m, 2560, VT_GROUPS[1], nb)
        vtc_m = _meta_vt(p_m, 3712, VT_GROUPS[2], nb)
        ya_r = _diff_real(p_r, vta_r, p_m, vta_m, lvecs, g_sub.reshape(LANES, 1), t5["a_far"], t5["a_tall"],
                          t5["a_biasm_t"], lam_init, nb)
        yb_r = _nat_real(p_r, vtb_r, p_m, vtb_m, nat_bias, nb)
        yc_r = _gqa_real(p_r, vtc_r, p_m, vtc_m, sink, t5["c_band_t"], t5["c_biasm_t"], nb)
        ya_m, yb_m, yc_m = _meta_queries(p_r, p_m, sink, lvecs, g_sub, t5["a_mq_real"], t5["a_mq_meta"],
                                         t5["c_mq_real"], t5["c_mq_meta"], lam_init, nb)
        h_r = _merge(h_r, norm_mix[l], ya_r, yb_r, yc_r, wg, wb, wo)
        h_m = _merge(h_m, norm_mix[l], ya_m, yb_m, yc_m, wg, wb, wo)

        last = l == depth - 1
        h_r = _ffn(h_r, norm_ffn2[l], w2i, w2o, final_gain=final_norm if last else None)
        h_m = _ffn(h_m, norm_ffn2[l], w2i, w2o)
        out = h_r
    return out.reshape(nb, SEQ, D_MODEL)
```

```python
import functools
import math

import numpy as np
import jax
import jax.numpy as jnp
from jax import lax
from jax.experimental import pallas as pl
from jax.experimental.pallas import tpu as pltpu

F32 = jnp.float32
BF16 = jnp.bfloat16

D_MODEL = 1024
SEQ = 4096
N_META = 16
GRID_W = 64
HEAD_DIM = 64
LANES = 128
A_HEADS = 4
B_HEADS = 8
C_HEADS = 8
C_KV_HEADS = 2
NA_ROWS = 8
NA_COLS = 16
WINDOW = 128
T5_BUCKETS = 32
D_FF = 2816
BRANCH_W = 512
IN_COLS = 3840
EPS = 1e-6
NEG = -1e30
SCALE = HEAD_DIM ** -0.5
LOG2E = math.log2(math.e)

QA_BLK, KA_BLK, VA_BLK, QB_BLK, KB_BLK, VB_BLK, QC_BLK = 0, 1, 2, 3, 4, 5, 6
KC_BLK128, VC_BLK128 = 28, 29
C_PERM = (0, 4, 1, 5, 2, 6, 3, 7)

TM_REAL = 1024
FF_CHUNK = 256
PROJ_CHUNK = 768
TQ = 256
TK = 256
TKD = 256
LOOKAHEAD = 2
DIFF_QB = 2
NAT_LOOKAHEAD = 3
VT_ROWS = 144
VT_GROUPS = (A_HEADS, B_HEADS // 2, 1)
VT_CHUNK = (TKD, 256, 128)
BIAS_TALL = 6 * TQ + TKD - TQ
NAT_R = 4
NAT_W = 12
GQA_KW = 512
VMEM_LIMIT = 56 * 1024 * 1024


def _cparams(n_axes, **kw):
    return pltpu.CompilerParams(dimension_semantics=("parallel",) * n_axes,
                                vmem_limit_bytes=VMEM_LIMIT, **kw)


def _const_spec(shape, n_grid):
    zeros = (0,) * len(shape)
    if n_grid == 1:
        return pl.BlockSpec(shape, lambda i: zeros, pipeline_mode=pl.Buffered(1))
    return pl.BlockSpec(shape, lambda i, j: zeros, pipeline_mode=pl.Buffered(1))


def _rms(x, g):
    return x * lax.rsqrt(jnp.mean(x * x, axis=-1, keepdims=True) + EPS) * g


def _dot(a, b):
    return jnp.dot(a, b, preferred_element_type=F32)


def _dot_t(a, b):
    return lax.dot_general(a, b, (((1,), (1,)), ((), ())), preferred_element_type=F32)


def _ffn_kernel(h_ref, g_ref, win_ref, wout_ref, *rest, final):
    if final:
        fg_ref, o_ref, act_ref = rest
    else:
        o_ref, act_ref = rest
    x = h_ref[...]
    xn = _rms(x, g_ref[...]).astype(BF16)
    for c in range(D_FF // FF_CHUNK):
        lo = c * FF_CHUNK
        gg = _dot(xn, win_ref[:, lo:lo + FF_CHUNK])
        uu = _dot(xn, win_ref[:, D_FF + lo:D_FF + lo + FF_CHUNK])
        act_ref[:, lo:lo + FF_CHUNK] = (gg * jax.nn.sigmoid(gg) * uu).astype(BF16)
    hn = x + 0.5 * _dot(act_ref[...], wout_ref[...])
    if final:
        hn = _rms(hn, fg_ref[...])
    o_ref[...] = hn


def _ffn(h, gain, w_in, w_out, final_gain=None):
    rows = h.shape[0]
    tm = min(TM_REAL, rows)
    final = final_gain is not None
    in_specs = [
        pl.BlockSpec((tm, D_MODEL), lambda i: (i, 0)),
        _const_spec((1, D_MODEL), 1),
        _const_spec((D_MODEL, 2 * D_FF), 1),
        _const_spec((D_FF, D_MODEL), 1),
    ]
    args = [h, gain.reshape(1, D_MODEL), w_in, w_out]
    if final:
        in_specs.append(_const_spec((1, D_MODEL), 1))
        args.append(final_gain.reshape(1, D_MODEL))
    return pl.pallas_call(
        functools.partial(_ffn_kernel, final=final),
        grid=(rows // tm,),
        in_specs=in_specs,
        out_specs=pl.BlockSpec((tm, D_MODEL), lambda i: (i, 0)),
        out_shape=jax.ShapeDtypeStruct((rows, D_MODEL), F32),
        scratch_shapes=[pltpu.VMEM((tm, D_FF), BF16)],
        compiler_params=_cparams(1),
        name="ffn",
    )(*args)


def _inproj_kernel(h_ref, g_ref, w_ref, *rest, transposed):
    if transposed:
        wvt_ref, o_ref, *vt_refs = rest
    else:
        (o_ref,) = rest
    xn = _rms(h_ref[...], g_ref[...]).astype(BF16)
    for c in range(IN_COLS // PROJ_CHUNK):
        lo = c * PROJ_CHUNK
        o_ref[:, lo:lo + PROJ_CHUNK] = _dot(xn, w_ref[:, lo:lo + PROJ_CHUNK]).astype(BF16)
    if transposed:
        tm = xn.shape[0]
        vt = _dot_t(wvt_ref[...], xn).astype(BF16)
        base = 0
        for vt_ref, groups, chunk in zip(vt_refs, VT_GROUPS, VT_CHUNK):
            for c in range(tm // chunk):
                cols = slice(c * chunk, (c + 1) * chunk)
                for g in range(groups):
                    vt_ref[c, g * VT_ROWS:g * VT_ROWS + LANES, :] = vt[base + g * LANES:base + (g + 1) * LANES, cols]
                    vt_ref[c, g * VT_ROWS + LANES:(g + 1) * VT_ROWS, :] = jnp.ones((VT_ROWS - LANES, chunk), BF16)
            base += groups * LANES


def _inproj(h, gain, w_in, wvt=None):
    rows = h.shape[0]
    tm = min(TM_REAL, rows)
    transposed = wvt is not None
    in_specs = [
        pl.BlockSpec((tm, D_MODEL), lambda i: (i, 0)),
        _const_spec((1, D_MODEL), 1),
        _const_spec((D_MODEL, IN_COLS), 1),
    ]
    args = [h, gain.reshape(1, D_MODEL), w_in]
    out_specs = [pl.BlockSpec((tm, IN_COLS), lambda i: (i, 0))]
    out_shape = [jax.ShapeDtypeStruct((rows, IN_COLS), BF16)]
    if transposed:
        in_specs.append(_const_spec(wvt.shape, 1))
        args.append(wvt)
        for groups, chunk in zip(VT_GROUPS, VT_CHUNK):
            out_specs.append(pl.BlockSpec((tm // chunk, groups * VT_ROWS, chunk), lambda i: (i, 0, 0)))
            out_shape.append(jax.ShapeDtypeStruct((rows // chunk, groups * VT_ROWS, chunk), BF16))
    res = pl.pallas_call(
        functools.partial(_inproj_kernel, transposed=transposed),
        grid=(rows // tm,),
        in_specs=in_specs,
        out_specs=out_specs,
        out_shape=out_shape,
        compiler_params=_cparams(1),
        name="inproj",
    )(*args)
    return res if transposed else res[0]


def _merge_kernel(h_ref, g_ref, ya_ref, yb_ref, yc_ref, wg_ref, wb_ref, wo_ref, o_ref):
    x = h_ref[...]
    xn = _rms(x, g_ref[...]).astype(BF16)
    merged = None
    for i, y_ref in enumerate((ya_ref, yb_ref, yc_ref)):
        term = jax.nn.sigmoid(_dot(xn, wg_ref[i])) * _dot(y_ref[...], wb_ref[i])
        merged = term if merged is None else merged + term
    o_ref[...] = x + _dot(merged.astype(BF16), wo_ref[...])


def _merge(h, gain, ya, yb, yc, w_gate, w_branch, w_out):
    rows = h.shape[0]
    tm = min(TM_REAL, rows)
    row_spec = lambda w: pl.BlockSpec((tm, w), lambda i: (i, 0))
    return pl.pallas_call(
        _merge_kernel,
        grid=(rows // tm,),
        in_specs=[
            row_spec(D_MODEL),
            _const_spec((1, D_MODEL), 1),
            row_spec(BRANCH_W), row_spec(BRANCH_W), row_spec(BRANCH_W),
            _const_spec((3, D_MODEL, D_MODEL), 1),
            _const_spec((3, BRANCH_W, D_MODEL), 1),
            _const_spec((D_MODEL, D_MODEL), 1),
        ],
        out_specs=row_spec(D_MODEL),
        out_shape=jax.ShapeDtypeStruct((rows, D_MODEL), F32),
        compiler_params=_cparams(1),
        name="merge",
    )(h, gain.reshape(1, D_MODEL), ya, yb, yc, w_gate, w_branch, w_out)


def _lane_lo(rows):
    return lax.broadcasted_iota(jnp.int32, (rows, LANES), 1) < HEAD_DIM


def _split_halves(q):
    lo = _lane_lo(q.shape[0])
    zero = jnp.zeros_like(q)
    return jnp.concatenate([jnp.where(lo, q, zero), jnp.where(lo, zero, q)], axis=0)


def _pad_rows(x, rows):
    return jnp.concatenate([x, jnp.zeros((rows - x.shape[0], x.shape[1]), x.dtype)], axis=0)


def _twice(b):
    return jnp.concatenate([b, b], axis=0)


def _side_by_side(a, b):
    return jnp.concatenate([a, b], axis=1)


def _lam(lq1_ref, lk1_ref, lq2_ref, lk2_ref, lam_init):
    s1 = jnp.sum(lq1_ref[...] * lk1_ref[...], axis=-1, keepdims=True)
    s2 = jnp.sum(lq2_ref[...] * lk2_ref[...], axis=-1, keepdims=True)
    return jnp.exp(s1) - jnp.exp(s2) + lam_init


def _diff_finish(o, lam, g, lam_init):
    m = o.shape[0] // 2
    d = o[:m] - lam * o[m:]
    return _rms(d, g) * (1.0 - lam_init)


def _pipeline(n, issue, consume, scr):
    ahead = len(scr) - 1
    pending = {}
    for t in range(min(ahead, n)):
        pending[t] = issue(t, scr[t % len(scr)])
    for t in range(n):
        if t + ahead < n:
            pending[t + ahead] = issue(t + ahead, scr[(t + ahead) % len(scr)])
        consume(t, scr[t % len(scr)], pending.pop(t))


def _pair_out(o):
    top = lax.broadcasted_iota(jnp.int32, (LANES, TQ), 0) < HEAD_DIM
    return jnp.where(top, o[:, :TQ], o[:, TQ:]).T


def _diff_real_kernel(far_ref, lq1_ref, lk1_ref, lq2_ref, lk2_ref, g_ref, q_ref, k_ref, vt_ref, km_ref, vmt_ref,
                      bias_ref, biasm_ref, o_ref, *scr, lam_init):
    lam = _lam(lq1_ref, lk1_ref, lq2_ref, lk2_ref, lam_init)
    g = g_ref[...]
    nk = k_ref.shape[0] // TKD
    ratio = TKD // TQ
    near_slots = 2 if ratio > 1 else 3

    def head_setup(qb, h):
        hs = slice(h * LANES, (h + 1) * LANES)
        q12 = _split_halves(q_ref[qb * TQ:(qb + 1) * TQ, hs])
        bm = biasm_ref[h, :, qb * TQ:(qb + 1) * TQ]
        s = _dot_t(km_ref[:, hs], q12) + _side_by_side(bm, bm)
        m0 = jnp.max(s, axis=0, keepdims=True)
        p = jnp.exp2(s - m0).astype(BF16)
        acc0 = _dot(vmt_ref[0, h * VT_ROWS:(h + 1) * VT_ROWS, :], p)
        return q12, m0, acc0

    def scores(qb, h, u, q12, dst_ref):
        i = DIFF_QB * pl.program_id(1) + qb
        jc = (i + ratio - 1) // ratio
        ju = jc - 1 + u
        j = jnp.where(ju >= nk, ju - nk, jnp.where(ju < 0, ju + nk, ju))
        kt = k_ref[pl.ds(pl.multiple_of(j * TKD, TKD), TKD), h * LANES:(h + 1) * LANES]
        s = _dot_t(kt, q12)
        if u < near_slots:
            e = jnp.clip(ratio * j - i, -3, 2) + 3
            bt = bias_ref[h, pl.ds(pl.multiple_of(e * TQ, TQ), TKD), :]
            s = s + _side_by_side(bt, bt)
            c = None
        else:
            c = jnp.where(ju >= nk, far_ref[0, h], far_ref[1, h])
        dst_ref[...] = s
        smax = jnp.max(s, axis=0, keepdims=True)
        return j, c, smax if c is None else smax + c

    def fold(h, src_ref, issued, m, acc):
        j, c, smax = issued
        m_new = jnp.maximum(m, smax)
        alpha = jnp.exp2(m - m_new)
        p = jnp.exp2(src_ref[...] - (m_new if c is None else m_new - c)).astype(BF16)
        return m_new, alpha * acc + _dot(vt_ref[j, h * VT_ROWS:(h + 1) * VT_ROWS, :], p)

    def finish(qb, h, acc):
        o = acc[:LANES] / acc[LANES:LANES + 1]
        d = o[:, :TQ] - lam * o[:, TQ:]
        y = d * lax.rsqrt(jnp.mean(d * d, axis=0, keepdims=True) + EPS) * g * (1.0 - lam_init)
        o_ref[qb * TQ:(qb + 1) * TQ, h * LANES:(h + 1) * LANES] = y.T.astype(BF16)

    tiles = [(qb, h, u) for qb in range(DIFF_QB) for h in range(A_HEADS) for u in range(nk)]
    setup = {}
    issued = {}

    def issue(t):
        qb, h, u = tiles[t]
        if (qb, h) not in setup:
            setup[qb, h] = head_setup(qb, h)
        issued[t] = scores(qb, h, u, setup[qb, h][0], scr[t % len(scr)])

    for t in range(LOOKAHEAD):
        issue(t)
    m = acc = None
    for t, (qb, h, u) in enumerate(tiles):
        if t + LOOKAHEAD < len(tiles):
            issue(t + LOOKAHEAD)
        if u == 0:
            _, m, acc = setup[qb, h]
        m, acc = fold(h, scr[t % len(scr)], issued.pop(t), m, acc)
        if u == nk - 1:
            finish(qb, h, acc)


def _diff_real(proj_r, vt, proj_m, vmt, lvecs, g_col, bias_far, bias_tall, biasm_t, lam_init, nb):
    nq = SEQ // (DIFF_QB * TQ)
    tq = DIFF_QB * TQ
    lspec = _const_spec((1, HEAD_DIM), 2)
    return pl.pallas_call(
        functools.partial(_diff_real_kernel, lam_init=lam_init),
        grid=(nb, nq),
        in_specs=[
            pl.BlockSpec(memory_space=pltpu.SMEM),
            lspec, lspec, lspec, lspec,
            _const_spec((LANES, 1), 2),
            pl.BlockSpec((tq, 512), lambda b, i: (b * nq + i, QA_BLK)),
            pl.BlockSpec((SEQ, 512), lambda b, i: (b, KA_BLK)),
            pl.BlockSpec((SEQ // TKD, A_HEADS * VT_ROWS, TKD), lambda b, i: (b, 0, 0)),
            pl.BlockSpec((N_META, 512), lambda b, i: (b, KA_BLK)),
            pl.BlockSpec((1, A_HEADS * VT_ROWS, N_META), lambda b, i: (b, 0, 0)),
            _const_spec((A_HEADS, BIAS_TALL, TQ), 2),
            pl.BlockSpec((A_HEADS, N_META, tq), lambda b, i: (0, 0, i)),
        ],
        out_specs=pl.BlockSpec((tq, 512), lambda b, i: (b * nq + i, 0)),
        out_shape=jax.ShapeDtypeStruct((nb * SEQ, BRANCH_W), BF16),
        scratch_shapes=[pltpu.VMEM((TKD, 2 * TQ), F32)] * (LOOKAHEAD + 1),
        compiler_params=_cparams(2),
        name="diff_real",
    )(bias_far, *lvecs, g_col, proj_r, proj_r, vt, proj_m, vmt, bias_tall, biasm_t)


def _nat_real_kernel(q_ref, k_ref, vt_ref, km_ref, vmt_ref, bias_ref, o_ref, *scr):
    blk = pl.program_id(1)
    chunk = vt_ref.shape[2]
    n_chunks = NAT_W * GRID_W // chunk
    c0 = jnp.clip(blk - 1, 0, vt_ref.shape[0] - n_chunks)
    start = pl.multiple_of(c0 * chunk, chunk)

    def scores(pr, dst_ref):
        hs = slice(pr * LANES, (pr + 1) * LANES)
        q2 = _split_halves(q_ref[:, hs])
        kw = k_ref[pl.ds(start, n_chunks * chunk), hs]
        s = _dot_t(kw, q2) + _side_by_side(bias_ref[0, 2 * pr], bias_ref[0, 2 * pr + 1])
        dst_ref[...] = s
        sm = _dot_t(km_ref[:, hs], q2)
        return sm, jnp.maximum(jnp.max(s, axis=0, keepdims=True), jnp.max(sm, axis=0, keepdims=True))

    def finish(pr, src_ref, pending):
        sm, m = pending
        rows = slice(pr * VT_ROWS, (pr + 1) * VT_ROWS)
        acc = _dot(vmt_ref[0, rows, :], jnp.exp2(sm - m).astype(BF16))
        for c in range(n_chunks):
            p = jnp.exp2(src_ref[c * chunk:(c + 1) * chunk, :] - m).astype(BF16)
            acc = acc + _dot(vt_ref[c0 + c, rows, :], p)
        o_ref[:, pr * LANES:(pr + 1) * LANES] = _pair_out(acc[:LANES] / acc[LANES:LANES + 1]).astype(BF16)

    _pipeline(B_HEADS // 2, scores, finish, scr)


def _nat_real(proj_r, vt, proj_m, vmt, bias_t, nb):
    nq = SEQ // TQ
    nkeys = NAT_W * GRID_W
    chunk = VT_CHUNK[1]

    def variant(b, i):
        return (jnp.where(i == 0, 0, jnp.where(i == nq - 1, 2, 1)), 0, 0, 0)

    return pl.pallas_call(
        _nat_real_kernel,
        grid=(nb, nq),
        in_specs=[
            pl.BlockSpec((TQ, 512), lambda b, i: (b * nq + i, QB_BLK)),
            pl.BlockSpec((SEQ, 512), lambda b, i: (b, KB_BLK)),
            pl.BlockSpec((SEQ // chunk, VT_GROUPS[1] * VT_ROWS, chunk), lambda b, i: (b, 0, 0)),
            pl.BlockSpec((N_META, 512), lambda b, i: (b, KB_BLK)),
            pl.BlockSpec((1, VT_GROUPS[1] * VT_ROWS, N_META), lambda b, i: (b, 0, 0)),
            pl.BlockSpec((1, B_HEADS, nkeys, TQ), variant),
        ],
        out_specs=pl.BlockSpec((TQ, 512), lambda b, i: (b * nq + i, 0)),
        out_shape=jax.ShapeDtypeStruct((nb * SEQ, BRANCH_W), BF16),
        scratch_shapes=[pltpu.VMEM((nkeys, 2 * TQ), F32)] * (NAT_LOOKAHEAD + 1),
        compiler_params=_cparams(2),
        name="nat_real",
    )(proj_r, proj_r, vt, proj_m, vmt, bias_t)


def _gqa_real_kernel(sink_ref, q_ref, k_ref, vt_ref, km_ref, vmt_ref, band_ref, mb_ref, o_ref, *scr):
    t = pl.program_id(1)
    chunk = vt_ref.shape[2]
    n_chunks = GQA_KW // chunk
    c0 = jnp.clip((TQ // chunk) * t - WINDOW // chunk, 0, vt_ref.shape[0] - n_chunks)
    kw = k_ref[pl.ds(pl.multiple_of(c0 * chunk, chunk), GQA_KW), :]
    vts = [_side_by_side(vt_ref[c0 + 2 * c], vt_ref[c0 + 2 * c + 1]) for c in range(n_chunks // 2)]
    npair = C_HEADS // 2

    def scores(pr, dst_ref):
        q2 = _split_halves(q_ref[:, pr * LANES:(pr + 1) * LANES])
        s = _dot_t(kw, q2) + _side_by_side(band_ref[0, pr], band_ref[0, npair + pr])
        dst_ref[...] = s
        sm = _dot_t(km_ref[...], q2) + _side_by_side(mb_ref[pr], mb_ref[npair + pr])
        sink = _side_by_side(jnp.full((1, TQ), sink_ref[pr], F32), jnp.full((1, TQ), sink_ref[npair + pr], F32))
        m = jnp.maximum(jnp.maximum(jnp.max(s, axis=0, keepdims=True), jnp.max(sm, axis=0, keepdims=True)), sink)
        return sm, sink, m

    def finish(pr, src_ref, pending):
        sm, sink, m = pending
        acc = _dot(vmt_ref[0], jnp.exp2(sm - m).astype(BF16))
        for c, vt2 in enumerate(vts):
            p = jnp.exp2(src_ref[2 * c * chunk:2 * (c + 1) * chunk, :] - m).astype(BF16)
            acc = acc + _dot(vt2, p)
        l = acc[LANES:LANES + 1] + jnp.exp2(sink - m)
        o_ref[:, pr * LANES:(pr + 1) * LANES] = _pair_out(acc[:LANES] / l).astype(BF16)

    _pipeline(npair, scores, finish, scr)


def _gqa_real(proj_r, vt, proj_m, vmt, sink, band_t, mbias_t, nb):
    nq = SEQ // TQ
    chunk = VT_CHUNK[2]

    def variant(b, i):
        return (jnp.where(i == 0, 0, jnp.where(i == nq - 1, 2, 1)), 0, 0, 0)

    return pl.pallas_call(
        _gqa_real_kernel,
        grid=(nb, nq),
        in_specs=[
            pl.BlockSpec(memory_space=pltpu.SMEM),
            pl.BlockSpec((TQ, 512), lambda b, i: (b * nq + i, QC_BLK)),
            pl.BlockSpec((SEQ, LANES), lambda b, i: (b, KC_BLK128)),
            pl.BlockSpec((SEQ // chunk, VT_ROWS, chunk), lambda b, i: (b, 0, 0)),
            pl.BlockSpec((N_META, LANES), lambda b, i: (b, KC_BLK128)),
            pl.BlockSpec((1, VT_ROWS, N_META), lambda b, i: (b, 0, 0)),
            pl.BlockSpec((1, C_HEADS, GQA_KW, TQ), variant),
            pl.BlockSpec((C_HEADS, N_META, TQ), lambda b, i: (0, 0, i)),
        ],
        out_specs=pl.BlockSpec((TQ, 512), lambda b, i: (b * nq + i, 0)),
        out_shape=jax.ShapeDtypeStruct((nb * SEQ, BRANCH_W), BF16),
        scratch_shapes=[pltpu.VMEM((GQA_KW, 2 * TQ), F32)] * (LOOKAHEAD + 1),
        compiler_params=_cparams(2),
        name="gqa_real",
    )(sink, proj_r, proj_r, vt, proj_m, vmt, band_t, mbias_t)


def _meta_kernel(sink_ref, lq1_ref, lk1_ref, lq2_ref, lk2_ref, g_ref, pm_ref, ka_ref, va_ref, kb_ref, vb_ref,
                 kc_ref, vc_ref, abr_ref, abm_ref, cbr_ref, cbm_ref, oa_ref, ob_ref, oc_ref, *, lam_init):
    lam = _lam(lq1_ref, lk1_ref, lq2_ref, lk2_ref, lam_init)
    g = g_ref[...]
    lo = _lane_lo(N_META)
    lane = lax.broadcasted_iota(jnp.int32, (1, LANES), 1)
    pad_bias = jnp.where(lane < N_META, 0.0, NEG)

    def col(blk512, sub):
        base = blk512 * 512 + sub * LANES
        return slice(base, base + LANES)

    for h in range(A_HEADS):
        hs = slice(h * LANES, (h + 1) * LANES)
        q12 = _split_halves(pm_ref[:, col(QA_BLK, h)])
        km = _pad_rows(pm_ref[:, col(KA_BLK, h)], LANES)
        vm = _pad_rows(pm_ref[:, col(VA_BLK, h)], LANES)
        s = _dot_t(q12, ka_ref[:, hs]) + _twice(abr_ref[h])
        sm = _dot_t(q12, km) + _twice(abm_ref[h])
        m = jnp.maximum(jnp.max(s, axis=-1, keepdims=True), jnp.max(sm, axis=-1, keepdims=True))
        e = jnp.exp2(s - m)
        em = jnp.exp2(sm - m)
        l = jnp.sum(e, axis=-1, keepdims=True) + jnp.sum(em, axis=-1, keepdims=True)
        o = (_dot(e.astype(BF16), va_ref[:, hs]) + _dot(em.astype(BF16), vm)) / l
        oa_ref[:, hs] = _diff_finish(o, lam, g, lam_init).astype(BF16)

    org = jnp.bitwise_and(lax.broadcasted_iota(jnp.int32, (1, NA_ROWS * GRID_W), 1), GRID_W - 1) < NA_COLS
    org_bias = jnp.where(org, 0.0, NEG)
    for pr in range(B_HEADS // 2):
        hs = slice(pr * LANES, (pr + 1) * LANES)
        q2 = _split_halves(pm_ref[:, col(QB_BLK, pr)])
        km = _pad_rows(pm_ref[:, col(KB_BLK, pr)], LANES)
        vm = _pad_rows(pm_ref[:, col(VB_BLK, pr)], LANES)
        s = _dot_t(q2, kb_ref[:, hs]) + org_bias
        sm = _dot_t(q2, km) + pad_bias
        m = jnp.maximum(jnp.max(s, axis=-1, keepdims=True), jnp.max(sm, axis=-1, keepdims=True))
        e = jnp.exp2(s - m)
        em = jnp.exp2(sm - m)
        l = jnp.sum(e, axis=-1, keepdims=True) + jnp.sum(em, axis=-1, keepdims=True)
        o = (_dot(e.astype(BF16), vb_ref[:, hs]) + _dot(em.astype(BF16), vm)) / l
        ob_ref[:, hs] = jnp.where(lo, o[:N_META], o[N_META:]).astype(BF16)

    kc_base = KC_BLK128 * LANES
    vc_base = VC_BLK128 * LANES
    km = _pad_rows(pm_ref[:, kc_base:kc_base + LANES], LANES)
    vm = _pad_rows(pm_ref[:, vc_base:vc_base + LANES], LANES)
    npair = C_HEADS // 2
    for pr in range(npair):
        hs = slice(pr * LANES, (pr + 1) * LANES)
        q2 = _split_halves(pm_ref[:, col(QC_BLK, pr)])
        s = _dot_t(q2, kc_ref[...]) + jnp.concatenate([cbr_ref[pr], cbr_ref[npair + pr]], axis=0)
        sm = _dot_t(q2, km) + jnp.concatenate([cbm_ref[pr], cbm_ref[npair + pr]], axis=0)
        sink = jnp.concatenate([jnp.full((N_META, 1), sink_ref[pr], F32),
                                jnp.full((N_META, 1), sink_ref[npair + pr], F32)], axis=0)
        m = jnp.maximum(jnp.maximum(jnp.max(s, axis=-1, keepdims=True), jnp.max(sm, axis=-1, keepdims=True)), sink)
        e = jnp.exp2(s - m)
        em = jnp.exp2(sm - m)
        l = jnp.sum(e, axis=-1, keepdims=True) + jnp.sum(em, axis=-1, keepdims=True) + jnp.exp2(sink - m)
        o = (_dot(e.astype(BF16), vc_ref[...]) + _dot(em.astype(BF16), vm)) / l
        oc_ref[:, hs] = jnp.where(lo, o[:N_META], o[N_META:]).astype(BF16)


def _meta_queries(proj_r, proj_m, sink, lvecs, g_sub, abr, abm, cbr, cbm, lam_init, nb):
    lspec = _const_spec((1, HEAD_DIM), 1)
    org_rows = NA_ROWS * GRID_W
    out = jax.ShapeDtypeStruct((nb * N_META, BRANCH_W), BF16)
    out_spec = pl.BlockSpec((N_META, BRANCH_W), lambda b: (b, 0))
    return pl.pallas_call(
        functools.partial(_meta_kernel, lam_init=lam_init),
        grid=(nb,),
        in_specs=[
            pl.BlockSpec(memory_space=pltpu.SMEM),
            lspec, lspec, lspec, lspec,
            _const_spec((1, LANES), 1),
            pl.BlockSpec((N_META, IN_COLS), lambda b: (b, 0)),
            pl.BlockSpec((SEQ, 512), lambda b: (b, KA_BLK)),
            pl.BlockSpec((SEQ, 512), lambda b: (b, VA_BLK)),
            pl.BlockSpec((org_rows, 512), lambda b: (b * (SEQ // org_rows), KB_BLK)),
            pl.BlockSpec((org_rows, 512), lambda b: (b * (SEQ // org_rows), VB_BLK)),
            pl.BlockSpec((WINDOW, LANES), lambda b: (b * (SEQ // WINDOW), KC_BLK128)),
            pl.BlockSpec((WINDOW, LANES), lambda b: (b * (SEQ // WINDOW), VC_BLK128)),
            _const_spec((A_HEADS, N_META, SEQ), 1),
            _const_spec((A_HEADS, N_META, LANES), 1),
            _const_spec((C_HEADS, N_META, WINDOW), 1),
            _const_spec((C_HEADS, N_META, LANES), 1),
        ],
        out_specs=[out_spec, out_spec, out_spec],
        out_shape=[out, out, out],
        compiler_params=_cparams(1),
        name="meta_queries",
    )(sink, *lvecs, g_sub, proj_m, proj_r, proj_r, proj_r, proj_r, proj_r, proj_r, abr, abm, cbr, cbm)


def _t5_bucket_np(rel):
    nb = T5_BUCKETS // 2
    max_exact = nb // 2
    rel = np.asarray(rel, np.int64)
    n = np.abs(rel)
    n2 = np.maximum(n, 1) ** 2
    large = np.minimum(np.floor(np.log2(n2.astype(np.float64))).astype(np.int64) + 2, nb - 1)
    return (np.where(rel > 0, nb, 0) + np.where(n < max_exact, n, large)).astype(np.int32)


def _t5_vals(table, rel):
    idx = _t5_bucket_np(rel)
    v = jnp.take(table.astype(F32), jnp.asarray(idx.reshape(-1)), axis=0)
    return jnp.moveaxis(v, -1, 0).reshape((table.shape[1],) + idx.shape)


def _t5_dense(table, rel):
    nb = T5_BUCKETS // 2
    max_exact = nb // 2
    n = jnp.abs(rel)
    steps = [int(math.ceil(math.sqrt(2.0 ** k))) for k in range(7, 7 + nb - 1 - max_exact)]
    large = max_exact + sum((n >= t).astype(jnp.int32) for t in steps)
    bucket = jnp.where(rel > 0, nb, 0) + jnp.where(n < max_exact, n, large)
    table = table.astype(F32)
    out = jnp.zeros((table.shape[1],) + rel.shape, F32)
    for b in range(T5_BUCKETS):
        out = jnp.where((bucket == b)[None], table[b].reshape((-1,) + (1,) * rel.ndim), out)
    return out


def _toeplitz(w, n, m):
    p = n + m - 1
    wp = jnp.concatenate([w, jnp.zeros(w.shape[:-1] + (1,), w.dtype)], axis=-1)
    flat = jnp.tile(wp, (1,) * (w.ndim - 1) + (n,))[..., :n * p]
    return flat.reshape(w.shape[:-1] + (n, p))[..., n - 1:n - 1 + m]


def _pad_lanes_neg(x):
    pad = jnp.full(x.shape[:-1] + (LANES - x.shape[-1],), NEG, F32)
    return jnp.concatenate([x, pad], axis=-1)


def _t5_tables(t5_table):
    table = t5_table.astype(F32) * LOG2E
    ta = table[:, :A_HEADS]
    tc = table[:, A_HEADS:]
    out = {}
    rel = (lax.broadcasted_iota(jnp.int32, (BIAS_TALL, TQ), 0) - 3 * TQ
           - lax.broadcasted_iota(jnp.int32, (BIAS_TALL, TQ), 1))
    out["a_tall"] = _t5_dense(ta, rel)
    out["a_far"] = _t5_vals(ta, np.array([-(SEQ + N_META), SEQ + N_META])).T
    rel_t = np.arange(N_META)[:, None] - (N_META + np.arange(TQ))[None, :]
    far_t = np.full((1, 1), -(SEQ + N_META))

    def meta_key_bias_t(tab):
        rest = jnp.broadcast_to(_t5_vals(tab, far_t), (tab.shape[1], N_META, SEQ - TQ))
        return jnp.concatenate([_t5_vals(tab, rel_t), rest], axis=-1)

    out["a_biasm_t"] = meta_key_bias_t(ta)
    out["c_biasm_t"] = meta_key_bias_t(tc)
    offs = np.array([0, -WINDOW, -(GQA_KW - TQ)])
    shape = (3, GQA_KW, TQ)
    relb = (lax.broadcasted_iota(jnp.int32, shape, 1) - lax.broadcasted_iota(jnp.int32, shape, 2)
            + jnp.asarray(offs, jnp.int32)[:, None, None])
    band = jnp.where((jnp.abs(relb) <= WINDOW)[None], _t5_dense(tc, relb), NEG)
    out["c_band_t"] = jnp.swapaxes(band, 0, 1)
    mq = np.arange(N_META)[:, None]
    near_k = N_META + np.arange(TK)[None, :] - mq
    far_k = np.full((1, 1), SEQ + N_META)
    first = _t5_vals(ta, near_k)
    rest = jnp.broadcast_to(_t5_vals(ta, far_k), (A_HEADS, N_META, SEQ - TK))
    out["a_mq_real"] = jnp.concatenate([first, rest], axis=-1)
    out["a_mq_meta"] = _pad_lanes_neg(_t5_vals(ta, np.arange(N_META)[None, :] - mq))
    rel0 = N_META + np.arange(WINDOW)[None, :] - mq
    out["c_mq_real"] = jnp.where(jnp.asarray(rel0 <= WINDOW)[None], _t5_vals(tc, rel0), NEG)
    out["c_mq_meta"] = _pad_lanes_neg(_t5_vals(tc, np.arange(N_META)[None, :] - mq))
    return out


def _nat_bias_t(rpb):
    n_rows = SEQ // GRID_W
    w = jnp.pad(rpb.astype(F32) * LOG2E, ((0, 0), (0, 0), (GRID_W - NA_COLS, GRID_W - NA_COLS)))
    tiles = _toeplitz(w, GRID_W, GRID_W)
    cols = np.arange(GRID_W)
    cstart = np.clip(cols - NA_COLS // 2, 0, GRID_W - NA_COLS)
    col_ok = (cols[None, :] >= cstart[:, None]) & (cols[None, :] < cstart[:, None] + NA_COLS)
    tiles = jnp.where(jnp.asarray(col_ok)[None, None], tiles, NEG)
    neg_tile = jnp.full((B_HEADS, 1, GRID_W, GRID_W), NEG, F32)
    tiles = jnp.concatenate([tiles, neg_tile], axis=1)
    n_blk = n_rows // NAT_R
    idx = np.zeros((3, NAT_R, NAT_W), np.int32)
    for v, blk in enumerate((0, 1, n_blk - 1)):
        r0 = NAT_R * blk
        w0 = int(np.clip(r0 - NA_ROWS // 2, 0, n_rows - NAT_W))
        for qr in range(NAT_R):
            r = r0 + qr
            rs = int(np.clip(r - NA_ROWS // 2, 0, n_rows - NA_ROWS))
            for kr in range(NAT_W):
                krow = w0 + kr
                idx[v, qr, kr] = krow - r + NA_ROWS - 1 if rs <= krow < rs + NA_ROWS else 2 * NA_ROWS - 1
    g = jnp.take(tiles, jnp.asarray(idx.reshape(-1)), axis=1)
    g = g.reshape(B_HEADS, 3, NAT_R, NAT_W, GRID_W, GRID_W)
    g = jnp.transpose(g, (1, 0, 3, 5, 2, 4))
    return g.reshape(3, B_HEADS, NAT_W * GRID_W, NAT_R * GRID_W)


def _prep_w_in(w):
    col = np.arange(IN_COLS)
    is_q = (col < 512) | ((col >= 1536) & (col < 2048)) | ((col >= 3072) & (col < 3584))
    w = w * jnp.asarray(np.where(is_q, SCALE * LOG2E, 1.0), F32)[None, :]
    qc = w[:, 3072:3584].reshape(D_MODEL, C_HEADS, HEAD_DIM)[:, np.asarray(C_PERM)].reshape(D_MODEL, 512)
    w = jnp.concatenate([w[:, :3072], qc, w[:, 3584:]], axis=1)
    return w.astype(BF16)


def _meta_vt(p_m, lo, groups, nb):
    v = p_m[:, lo:lo + groups * LANES].reshape(nb, N_META, groups, LANES)
    v = jnp.transpose(v, (0, 2, 3, 1))
    ones = jnp.ones((nb, groups, VT_ROWS - LANES, N_META), BF16)
    return jnp.concatenate([v, ones], axis=2).reshape(nb, groups * VT_ROWS, N_META)


def kernel(x, meta_tokens, t5_table, norm_ffn1, w_ffn1_in, w_ffn1_out, norm_mix, w_in, lambda_q1, lambda_k1,
           lambda_q2, lambda_k2, subln_gain, natten_rpb, sink_logits, w_branch, w_gate, w_out, norm_ffn2,
           w_ffn2_in, w_ffn2_out, final_norm):
    nb, seq, d = x.shape
    assert (seq, d) == (SEQ, D_MODEL)
    depth = norm_ffn1.shape[0]
    h_r = x.reshape(nb * SEQ, D_MODEL)
    h_m = jnp.broadcast_to(meta_tokens[None].astype(x.dtype), (nb, N_META, D_MODEL)).reshape(nb * N_META, D_MODEL)
    t5 = _t5_tables(t5_table)
    perm = np.asarray(C_PERM)
    out = None
    for l in range(depth):
        lam_init = 0.8 - 0.6 * math.exp(-0.3 * l)
        w1i, w1o = w_ffn1_in[l].astype(BF16), w_ffn1_out[l].astype(BF16)
        w2i, w2o = w_ffn2_in[l].astype(BF16), w_ffn2_out[l].astype(BF16)
        wi = _prep_w_in(w_in[l])
        wvt = jnp.concatenate([wi[:, 1024:1536], wi[:, 2560:3072], wi[:, 3712:3840]], axis=1).T
        wg = w_gate[l].astype(BF16)
        wb2 = w_branch[l, 2].reshape(C_HEADS, HEAD_DIM, D_MODEL)[perm].reshape(BRANCH_W, D_MODEL)
        wb = jnp.stack([w_branch[l, 0], w_branch[l, 1], wb2]).astype(BF16)
        wo = w_out[l].astype(BF16)
        lvecs = [v[l].reshape(1, HEAD_DIM).astype(F32) for v in (lambda_q1, lambda_k1, lambda_q2, lambda_k2)]
        g_sub = subln_gain[l].reshape(1, LANES).astype(F32)
        sink = sink_logits[l].astype(F32)[perm] * LOG2E
        nat_bias = _nat_bias_t(natten_rpb[l])

        h_r = _ffn(h_r, norm_ffn1[l], w1i, w1o)
        h_m = _ffn(h_m, norm_ffn1[l], w1i, w1o)

        p_r, vta_r, vtb_r, vtc_r = _inproj(h_r, norm_mix[l], wi, wvt)
        p_m = _inproj(h_m, norm_mix[l], wi)
        vta_m = _meta_vt(p_m, 1024, VT_GROUPS[0], nb)
        vtb_m = _meta_vt(p_m, 2560, VT_GROUPS[1], nb)
        vtc_m = _meta_vt(p_m, 3712, VT_GROUPS[2], nb)
        ya_r = _diff_real(p_r, vta_r, p_m, vta_m, lvecs, g_sub.reshape(LANES, 1), t5["a_far"], t5["a_tall"],
                          t5["a_biasm_t"], lam_init, nb)
        yb_r = _nat_real(p_r, vtb_r, p_m, vtb_m, nat_bias, nb)
        yc_r = _gqa_real(p_r, vtc_r, p_m, vtc_m, sink, t5["c_band_t"], t5["c_biasm_t"], nb)
        ya_m, yb_m, yc_m = _meta_queries(p_r, p_m, sink, lvecs, g_sub, t5["a_mq_real"], t5["a_mq_meta"],
                                         t5["c_mq_real"], t5["c_mq_meta"], lam_init, nb)
        h_r = _merge(h_r, norm_mix[l], ya_r, yb_r, yc_r, wg, wb, wo)
        h_m = _merge(h_m, norm_mix[l], ya_m, yb_m, yc_m, wg, wb, wo)

        last = l == depth - 1
        h_r = _ffn(h_r, norm_ffn2[l], w2i, w2o, final_gain=final_norm if last else None)
        h_m = _ffn(h_m, norm_ffn2[l], w2i, w2o)
        out = h_r
    return out.reshape(nb, SEQ, D_MODEL)
```

```python
import functools
import math

import numpy as np
import jax
import jax.numpy as jnp
from jax import lax
from jax.experimental import pallas as pl
from jax.experimental.pallas import tpu as pltpu

F32 = jnp.float32
BF16 = jnp.bfloat16

D_MODEL = 1024
SEQ = 4096
N_META = 16
GRID_W = 64
HEAD_DIM = 64
LANES = 128
A_HEADS = 4
B_HEADS = 8
C_HEADS = 8
C_KV_HEADS = 2
NA_ROWS = 8
NA_COLS = 16
WINDOW = 128
T5_BUCKETS = 32
D_FF = 2816
BRANCH_W = 512
IN_COLS = 3840
EPS = 1e-6
NEG = -1e30
SCALE = HEAD_DIM ** -0.5
LOG2E = math.log2(math.e)

QA_BLK, KA_BLK, VA_BLK, QB_BLK, KB_BLK, VB_BLK, QC_BLK = 0, 1, 2, 3, 4, 5, 6
KC_BLK128, VC_BLK128 = 28, 29
C_PERM = (0, 4, 1, 5, 2, 6, 3, 7)

TM_REAL = 1024
FF_CHUNK = 256
PROJ_CHUNK = 768
TQ = 256
TK = 256
TKD = 256
LOOKAHEAD = 2
DIFF_QB = 2
LOCAL_QB = 2
NAT_LOOKAHEAD = 3
VT_ROWS = 144
VT_GROUPS = (A_HEADS, B_HEADS // 2, 1)
VT_CHUNK = (TKD, 256, 128)
BIAS_TALL = 6 * TQ + TKD - TQ
NAT_R = 4
NAT_W = 12
GQA_KW = 512
VMEM_LIMIT = 56 * 1024 * 1024


def _cparams(n_axes, **kw):
    return pltpu.CompilerParams(dimension_semantics=("parallel",) * n_axes,
                                vmem_limit_bytes=VMEM_LIMIT, **kw)


def _const_spec(shape, n_grid):
    zeros = (0,) * len(shape)
    if n_grid == 1:
        return pl.BlockSpec(shape, lambda i: zeros, pipeline_mode=pl.Buffered(1))
    return pl.BlockSpec(shape, lambda i, j: zeros, pipeline_mode=pl.Buffered(1))


def _rms(x, g):
    return x * lax.rsqrt(jnp.mean(x * x, axis=-1, keepdims=True) + EPS) * g


def _dot(a, b):
    return jnp.dot(a, b, preferred_element_type=F32)


def _dot_t(a, b):
    return lax.dot_general(a, b, (((1,), (1,)), ((), ())), preferred_element_type=F32)


def _ffn_kernel(h_ref, g_ref, win_ref, wout_ref, *rest, final):
    if final:
        fg_ref, o_ref, act_ref = rest
    else:
        o_ref, act_ref = rest
    x = h_ref[...]
    xn = _rms(x, g_ref[...]).astype(BF16)
    for c in range(D_FF // FF_CHUNK):
        lo = c * FF_CHUNK
        gg = _dot(xn, win_ref[:, lo:lo + FF_CHUNK])
        uu = _dot(xn, win_ref[:, D_FF + lo:D_FF + lo + FF_CHUNK])
        act_ref[:, lo:lo + FF_CHUNK] = (gg * jax.nn.sigmoid(gg) * uu).astype(BF16)
    hn = x + 0.5 * _dot(act_ref[...], wout_ref[...])
    if final:
        hn = _rms(hn, fg_ref[...])
    o_ref[...] = hn


def _ffn(h, gain, w_in, w_out, final_gain=None):
    rows = h.shape[0]
    tm = min(TM_REAL, rows)
    final = final_gain is not None
    in_specs = [
        pl.BlockSpec((tm, D_MODEL), lambda i: (i, 0)),
        _const_spec((1, D_MODEL), 1),
        _const_spec((D_MODEL, 2 * D_FF), 1),
        _const_spec((D_FF, D_MODEL), 1),
    ]
    args = [h, gain.reshape(1, D_MODEL), w_in, w_out]
    if final:
        in_specs.append(_const_spec((1, D_MODEL), 1))
        args.append(final_gain.reshape(1, D_MODEL))
    return pl.pallas_call(
        functools.partial(_ffn_kernel, final=final),
        grid=(rows // tm,),
        in_specs=in_specs,
        out_specs=pl.BlockSpec((tm, D_MODEL), lambda i: (i, 0)),
        out_shape=jax.ShapeDtypeStruct((rows, D_MODEL), F32),
        scratch_shapes=[pltpu.VMEM((tm, D_FF), BF16)],
        compiler_params=_cparams(1),
        name="ffn",
    )(*args)


def _inproj_kernel(h_ref, g_ref, w_ref, *rest, transposed):
    if transposed:
        wvt_ref, o_ref, *vt_refs = rest
    else:
        (o_ref,) = rest
    xn = _rms(h_ref[...], g_ref[...]).astype(BF16)
    for c in range(IN_COLS // PROJ_CHUNK):
        lo = c * PROJ_CHUNK
        o_ref[:, lo:lo + PROJ_CHUNK] = _dot(xn, w_ref[:, lo:lo + PROJ_CHUNK]).astype(BF16)
    if transposed:
        tm = xn.shape[0]
        vt = _dot_t(wvt_ref[...], xn).astype(BF16)
        base = 0
        for vt_ref, groups, chunk in zip(vt_refs, VT_GROUPS, VT_CHUNK):
            for c in range(tm // chunk):
                cols = slice(c * chunk, (c + 1) * chunk)
                for g in range(groups):
                    vt_ref[c, g * VT_ROWS:g * VT_ROWS + LANES, :] = vt[base + g * LANES:base + (g + 1) * LANES, cols]
                    vt_ref[c, g * VT_ROWS + LANES:(g + 1) * VT_ROWS, :] = jnp.ones((VT_ROWS - LANES, chunk), BF16)
            base += groups * LANES


def _inproj(h, gain, w_in, wvt=None):
    rows = h.shape[0]
    tm = min(TM_REAL, rows)
    transposed = wvt is not None
    in_specs = [
        pl.BlockSpec((tm, D_MODEL), lambda i: (i, 0)),
        _const_spec((1, D_MODEL), 1),
        _const_spec((D_MODEL, IN_COLS), 1),
    ]
    args = [h, gain.reshape(1, D_MODEL), w_in]
    out_specs = [pl.BlockSpec((tm, IN_COLS), lambda i: (i, 0))]
    out_shape = [jax.ShapeDtypeStruct((rows, IN_COLS), BF16)]
    if transposed:
        in_specs.append(_const_spec(wvt.shape, 1))
        args.append(wvt)
        for groups, chunk in zip(VT_GROUPS, VT_CHUNK):
            out_specs.append(pl.BlockSpec((tm // chunk, groups * VT_ROWS, chunk), lambda i: (i, 0, 0)))
            out_shape.append(jax.ShapeDtypeStruct((rows // chunk, groups * VT_ROWS, chunk), BF16))
    res = pl.pallas_call(
        functools.partial(_inproj_kernel, transposed=transposed),
        grid=(rows // tm,),
        in_specs=in_specs,
        out_specs=out_specs,
        out_shape=out_shape,
        compiler_params=_cparams(1),
        name="inproj",
    )(*args)
    return res if transposed else res[0]


def _merge_kernel(h_ref, g_ref, ya_ref, yb_ref, yc_ref, wg_ref, wb_ref, wo_ref, o_ref):
    x = h_ref[...]
    xn = _rms(x, g_ref[...]).astype(BF16)
    merged = None
    for i, y_ref in enumerate((ya_ref, yb_ref, yc_ref)):
        term = jax.nn.sigmoid(_dot(xn, wg_ref[i])) * _dot(y_ref[...], wb_ref[i])
        merged = term if merged is None else merged + term
    o_ref[...] = x + _dot(merged.astype(BF16), wo_ref[...])


def _merge(h, gain, ya, yb, yc, w_gate, w_branch, w_out):
    rows = h.shape[0]
    tm = min(TM_REAL, rows)
    row_spec = lambda w: pl.BlockSpec((tm, w), lambda i: (i, 0))
    return pl.pallas_call(
        _merge_kernel,
        grid=(rows // tm,),
        in_specs=[
            row_spec(D_MODEL),
            _const_spec((1, D_MODEL), 1),
            row_spec(BRANCH_W), row_spec(BRANCH_W), row_spec(BRANCH_W),
            _const_spec((3, D_MODEL, D_MODEL), 1),
            _const_spec((3, BRANCH_W, D_MODEL), 1),
            _const_spec((D_MODEL, D_MODEL), 1),
        ],
        out_specs=row_spec(D_MODEL),
        out_shape=jax.ShapeDtypeStruct((rows, D_MODEL), F32),
        compiler_params=_cparams(1),
        name="merge",
    )(h, gain.reshape(1, D_MODEL), ya, yb, yc, w_gate, w_branch, w_out)


def _lane_lo(rows):
    return lax.broadcasted_iota(jnp.int32, (rows, LANES), 1) < HEAD_DIM


def _split_halves(q):
    lo = _lane_lo(q.shape[0])
    zero = jnp.zeros_like(q)
    return jnp.concatenate([jnp.where(lo, q, zero), jnp.where(lo, zero, q)], axis=0)


def _pad_rows(x, rows):
    return jnp.concatenate([x, jnp.zeros((rows - x.shape[0], x.shape[1]), x.dtype)], axis=0)


def _twice(b):
    return jnp.concatenate([b, b], axis=0)


def _side_by_side(a, b):
    return jnp.concatenate([a, b], axis=1)


def _lam(lq1_ref, lk1_ref, lq2_ref, lk2_ref, lam_init):
    s1 = jnp.sum(lq1_ref[...] * lk1_ref[...], axis=-1, keepdims=True)
    s2 = jnp.sum(lq2_ref[...] * lk2_ref[...], axis=-1, keepdims=True)
    return jnp.exp(s1) - jnp.exp(s2) + lam_init


def _diff_finish(o, lam, g, lam_init):
    m = o.shape[0] // 2
    d = o[:m] - lam * o[m:]
    return _rms(d, g) * (1.0 - lam_init)


def _pipeline(n, issue, consume, scr):
    ahead = len(scr) - 1
    pending = {}
    for t in range(min(ahead, n)):
        pending[t] = issue(t, scr[t % len(scr)])
    for t in range(n):
        if t + ahead < n:
            pending[t + ahead] = issue(t + ahead, scr[(t + ahead) % len(scr)])
        consume(t, scr[t % len(scr)], pending.pop(t))


def _pair_out(o):
    top = lax.broadcasted_iota(jnp.int32, (LANES, TQ), 0) < HEAD_DIM
    return jnp.where(top, o[:, :TQ], o[:, TQ:]).T


def _diff_real_kernel(far_ref, lq1_ref, lk1_ref, lq2_ref, lk2_ref, g_ref, q_ref, k_ref, vt_ref, km_ref, vmt_ref,
                      bias_ref, biasm_ref, o_ref, *scr, lam_init):
    lam = _lam(lq1_ref, lk1_ref, lq2_ref, lk2_ref, lam_init)
    g = g_ref[...]
    nk = k_ref.shape[0] // TKD
    ratio = TKD // TQ
    near_slots = 2 if ratio > 1 else 3

    def head_setup(qb, h):
        hs = slice(h * LANES, (h + 1) * LANES)
        q12 = _split_halves(q_ref[qb * TQ:(qb + 1) * TQ, hs])
        bm = biasm_ref[h, :, qb * TQ:(qb + 1) * TQ]
        s = _dot_t(km_ref[:, hs], q12) + _side_by_side(bm, bm)
        m0 = jnp.max(s, axis=0, keepdims=True)
        p = jnp.exp2(s - m0).astype(BF16)
        acc0 = _dot(vmt_ref[0, h * VT_ROWS:(h + 1) * VT_ROWS, :], p)
        return q12, m0, acc0

    def scores(qb, h, u, q12, dst_ref):
        i = DIFF_QB * pl.program_id(1) + qb
        jc = (i + ratio - 1) // ratio
        ju = jc - 1 + u
        j = jnp.where(ju >= nk, ju - nk, jnp.where(ju < 0, ju + nk, ju))
        kt = k_ref[pl.ds(pl.multiple_of(j * TKD, TKD), TKD), h * LANES:(h + 1) * LANES]
        s = _dot_t(kt, q12)
        if u < near_slots:
            e = jnp.clip(ratio * j - i, -3, 2) + 3
            bt = bias_ref[h, pl.ds(pl.multiple_of(e * TQ, TQ), TKD), :]
            s = s + _side_by_side(bt, bt)
            c = None
        else:
            c = jnp.where(ju >= nk, far_ref[0, h], far_ref[1, h])
        dst_ref[...] = s
        smax = jnp.max(s, axis=0, keepdims=True)
        return j, c, smax if c is None else smax + c

    def fold(h, src_ref, issued, m, acc):
        j, c, smax = issued
        m_new = jnp.maximum(m, smax)
        alpha = jnp.exp2(m - m_new)
        p = jnp.exp2(src_ref[...] - (m_new if c is None else m_new - c)).astype(BF16)
        return m_new, alpha * acc + _dot(vt_ref[j, h * VT_ROWS:(h + 1) * VT_ROWS, :], p)

    def finish(qb, h, acc):
        o = acc[:LANES] / acc[LANES:LANES + 1]
        d = o[:, :TQ] - lam * o[:, TQ:]
        y = d * lax.rsqrt(jnp.mean(d * d, axis=0, keepdims=True) + EPS) * g * (1.0 - lam_init)
        o_ref[qb * TQ:(qb + 1) * TQ, h * LANES:(h + 1) * LANES] = y.T.astype(BF16)

    tiles = [(qb, h, u) for qb in range(DIFF_QB) for h in range(A_HEADS) for u in range(nk)]
    setup = {}
    issued = {}

    def issue(t):
        qb, h, u = tiles[t]
        if (qb, h) not in setup:
            setup[qb, h] = head_setup(qb, h)
        issued[t] = scores(qb, h, u, setup[qb, h][0], scr[t % len(scr)])

    for t in range(LOOKAHEAD):
        issue(t)
    m = acc = None
    for t, (qb, h, u) in enumerate(tiles):
        if t + LOOKAHEAD < len(tiles):
            issue(t + LOOKAHEAD)
        if u == 0:
            _, m, acc = setup[qb, h]
        m, acc = fold(h, scr[t % len(scr)], issued.pop(t), m, acc)
        if u == nk - 1:
            finish(qb, h, acc)


def _diff_real(proj_r, vt, proj_m, vmt, lvecs, g_col, bias_far, bias_tall, biasm_t, lam_init, nb):
    nq = SEQ // (DIFF_QB * TQ)
    tq = DIFF_QB * TQ
    lspec = _const_spec((1, HEAD_DIM), 2)
    return pl.pallas_call(
        functools.partial(_diff_real_kernel, lam_init=lam_init),
        grid=(nb, nq),
        in_specs=[
            pl.BlockSpec(memory_space=pltpu.SMEM),
            lspec, lspec, lspec, lspec,
            _const_spec((LANES, 1), 2),
            pl.BlockSpec((tq, 512), lambda b, i: (b * nq + i, QA_BLK)),
            pl.BlockSpec((SEQ, 512), lambda b, i: (b, KA_BLK)),
            pl.BlockSpec((SEQ // TKD, A_HEADS * VT_ROWS, TKD), lambda b, i: (b, 0, 0)),
            pl.BlockSpec((N_META, 512), lambda b, i: (b, KA_BLK)),
            pl.BlockSpec((1, A_HEADS * VT_ROWS, N_META), lambda b, i: (b, 0, 0)),
            _const_spec((A_HEADS, BIAS_TALL, TQ), 2),
            pl.BlockSpec((A_HEADS, N_META, tq), lambda b, i: (0, 0, i)),
        ],
        out_specs=pl.BlockSpec((tq, 512), lambda b, i: (b * nq + i, 0)),
        out_shape=jax.ShapeDtypeStruct((nb * SEQ, BRANCH_W), BF16),
        scratch_shapes=[pltpu.VMEM((TKD, 2 * TQ), F32)] * (LOOKAHEAD + 1),
        compiler_params=_cparams(2),
        name="diff_real",
    )(bias_far, *lvecs, g_col, proj_r, proj_r, vt, proj_m, vmt, bias_tall, biasm_t)


def _nat_real_kernel(q_ref, k_ref, vt_ref, km_ref, vmt_ref, bias_ref, o_ref, *scr):
    chunk = vt_ref.shape[2]
    n_chunks = NAT_W * GRID_W // chunk
    n_blk = vt_ref.shape[0] * chunk // TQ
    npair = B_HEADS // 2

    def block(qb):
        blk = LOCAL_QB * pl.program_id(1) + qb
        c0 = jnp.clip(blk - 1, 0, vt_ref.shape[0] - n_chunks)
        return c0, jnp.where(blk == 0, 0, jnp.where(blk == n_blk - 1, 2, 1))

    ctx = [block(qb) for qb in range(LOCAL_QB)]

    def scores(t, dst_ref):
        qb, pr = divmod(t, npair)
        c0, variant = ctx[qb]
        hs = slice(pr * LANES, (pr + 1) * LANES)
        q2 = _split_halves(q_ref[qb * TQ:(qb + 1) * TQ, hs])
        kw = k_ref[pl.ds(pl.multiple_of(c0 * chunk, chunk), n_chunks * chunk), hs]
        s = _dot_t(kw, q2) + _side_by_side(bias_ref[variant, 2 * pr], bias_ref[variant, 2 * pr + 1])
        dst_ref[...] = s
        sm = _dot_t(km_ref[:, hs], q2)
        return sm, jnp.maximum(jnp.max(s, axis=0, keepdims=True), jnp.max(sm, axis=0, keepdims=True))

    def finish(t, src_ref, pending):
        qb, pr = divmod(t, npair)
        c0, _ = ctx[qb]
        sm, m = pending
        rows = slice(pr * VT_ROWS, (pr + 1) * VT_ROWS)
        acc = _dot(vmt_ref[0, rows, :], jnp.exp2(sm - m).astype(BF16))
        for c in range(n_chunks):
            p = jnp.exp2(src_ref[c * chunk:(c + 1) * chunk, :] - m).astype(BF16)
            acc = acc + _dot(vt_ref[c0 + c, rows, :], p)
        out = _pair_out(acc[:LANES] / acc[LANES:LANES + 1]).astype(BF16)
        o_ref[qb * TQ:(qb + 1) * TQ, pr * LANES:(pr + 1) * LANES] = out

    _pipeline(LOCAL_QB * npair, scores, finish, scr)


def _nat_real(proj_r, vt, proj_m, vmt, bias_t, nb):
    tq = LOCAL_QB * TQ
    nq = SEQ // tq
    nkeys = NAT_W * GRID_W
    chunk = VT_CHUNK[1]
    return pl.pallas_call(
        _nat_real_kernel,
        grid=(nb, nq),
        in_specs=[
            pl.BlockSpec((tq, 512), lambda b, i: (b * nq + i, QB_BLK)),
            pl.BlockSpec((SEQ, 512), lambda b, i: (b, KB_BLK)),
            pl.BlockSpec((SEQ // chunk, VT_GROUPS[1] * VT_ROWS, chunk), lambda b, i: (b, 0, 0)),
            pl.BlockSpec((N_META, 512), lambda b, i: (b, KB_BLK)),
            pl.BlockSpec((1, VT_GROUPS[1] * VT_ROWS, N_META), lambda b, i: (b, 0, 0)),
            _const_spec((3, B_HEADS, nkeys, TQ), 2),
        ],
        out_specs=pl.BlockSpec((tq, 512), lambda b, i: (b * nq + i, 0)),
        out_shape=jax.ShapeDtypeStruct((nb * SEQ, BRANCH_W), BF16),
        scratch_shapes=[pltpu.VMEM((nkeys, 2 * TQ), F32)] * (NAT_LOOKAHEAD + 1),
        compiler_params=_cparams(2),
        name="nat_real",
    )(proj_r, proj_r, vt, proj_m, vmt, bias_t)


def _gqa_real_kernel(sink_ref, q_ref, k_ref, vt_ref, km_ref, vmt_ref, band_ref, mb_ref, o_ref, *scr):
    chunk = vt_ref.shape[2]
    n_chunks = GQA_KW // chunk
    n_blk = vt_ref.shape[0] * chunk // TQ
    npair = C_HEADS // 2

    def block(qb):
        t = LOCAL_QB * pl.program_id(1) + qb
        c0 = jnp.clip((TQ // chunk) * t - WINDOW // chunk, 0, vt_ref.shape[0] - n_chunks)
        return c0, jnp.where(t == 0, 0, jnp.where(t == n_blk - 1, 2, 1))

    ctx = [block(qb) for qb in range(LOCAL_QB)]

    def scores(t, dst_ref):
        qb, pr = divmod(t, npair)
        c0, variant = ctx[qb]
        cols = slice(qb * TQ, (qb + 1) * TQ)
        q2 = _split_halves(q_ref[cols, pr * LANES:(pr + 1) * LANES])
        kw = k_ref[pl.ds(pl.multiple_of(c0 * chunk, chunk), GQA_KW), :]
        s = _dot_t(kw, q2) + _side_by_side(band_ref[variant, pr], band_ref[variant, npair + pr])
        dst_ref[...] = s
        sm = _dot_t(km_ref[...], q2) + _side_by_side(mb_ref[pr, :, cols], mb_ref[npair + pr, :, cols])
        sink = _side_by_side(jnp.full((1, TQ), sink_ref[pr], F32), jnp.full((1, TQ), sink_ref[npair + pr], F32))
        m = jnp.maximum(jnp.maximum(jnp.max(s, axis=0, keepdims=True), jnp.max(sm, axis=0, keepdims=True)), sink)
        return sm, sink, m

    def finish(t, src_ref, pending):
        qb, pr = divmod(t, npair)
        c0, _ = ctx[qb]
        sm, sink, m = pending
        acc = _dot(vmt_ref[0], jnp.exp2(sm - m).astype(BF16))
        for c in range(n_chunks // 2):
            vt2 = _side_by_side(vt_ref[c0 + 2 * c], vt_ref[c0 + 2 * c + 1])
            p = jnp.exp2(src_ref[2 * c * chunk:2 * (c + 1) * chunk, :] - m).astype(BF16)
            acc = acc + _dot(vt2, p)
        l = acc[LANES:LANES + 1] + jnp.exp2(sink - m)
        o_ref[qb * TQ:(qb + 1) * TQ, pr * LANES:(pr + 1) * LANES] = _pair_out(acc[:LANES] / l).astype(BF16)

    _pipeline(LOCAL_QB * npair, scores, finish, scr)


def _gqa_real(proj_r, vt, proj_m, vmt, sink, band_t, mbias_t, nb):
    tq = LOCAL_QB * TQ
    nq = SEQ // tq
    chunk = VT_CHUNK[2]
    return pl.pallas_call(
        _gqa_real_kernel,
        grid=(nb, nq),
        in_specs=[
            pl.BlockSpec(memory_space=pltpu.SMEM),
            pl.BlockSpec((tq, 512), lambda b, i: (b * nq + i, QC_BLK)),
            pl.BlockSpec((SEQ, LANES), lambda b, i: (b, KC_BLK128)),
            pl.BlockSpec((SEQ // chunk, VT_ROWS, chunk), lambda b, i: (b, 0, 0)),
            pl.BlockSpec((N_META, LANES), lambda b, i: (b, KC_BLK128)),
            pl.BlockSpec((1, VT_ROWS, N_META), lambda b, i: (b, 0, 0)),
            _const_spec((3, C_HEADS, GQA_KW, TQ), 2),
            pl.BlockSpec((C_HEADS, N_META, tq), lambda b, i: (0, 0, i)),
        ],
        out_specs=pl.BlockSpec((tq, 512), lambda b, i: (b * nq + i, 0)),
        out_shape=jax.ShapeDtypeStruct((nb * SEQ, BRANCH_W), BF16),
        scratch_shapes=[pltpu.VMEM((GQA_KW, 2 * TQ), F32)] * (LOOKAHEAD + 1),
        compiler_params=_cparams(2),
        name="gqa_real",
    )(sink, proj_r, proj_r, vt, proj_m, vmt, band_t, mbias_t)


def _meta_kernel(sink_ref, lq1_ref, lk1_ref, lq2_ref, lk2_ref, g_ref, pm_ref, ka_ref, va_ref, kb_ref, vb_ref,
                 kc_ref, vc_ref, abr_ref, abm_ref, cbr_ref, cbm_ref, oa_ref, ob_ref, oc_ref, *, lam_init):
    lam = _lam(lq1_ref, lk1_ref, lq2_ref, lk2_ref, lam_init)
    g = g_ref[...]
    lo = _lane_lo(N_META)
    lane = lax.broadcasted_iota(jnp.int32, (1, LANES), 1)
    pad_bias = jnp.where(lane < N_META, 0.0, NEG)

    def col(blk512, sub):
        base = blk512 * 512 + sub * LANES
        return slice(base, base + LANES)

    for h in range(A_HEADS):
        hs = slice(h * LANES, (h + 1) * LANES)
        q12 = _split_halves(pm_ref[:, col(QA_BLK, h)])
        km = _pad_rows(pm_ref[:, col(KA_BLK, h)], LANES)
        vm = _pad_rows(pm_ref[:, col(VA_BLK, h)], LANES)
        s = _dot_t(q12, ka_ref[:, hs]) + _twice(abr_ref[h])
        sm = _dot_t(q12, km) + _twice(abm_ref[h])
        m = jnp.maximum(jnp.max(s, axis=-1, keepdims=True), jnp.max(sm, axis=-1, keepdims=True))
        e = jnp.exp2(s - m)
        em = jnp.exp2(sm - m)
        l = jnp.sum(e, axis=-1, keepdims=True) + jnp.sum(em, axis=-1, keepdims=True)
        o = (_dot(e.astype(BF16), va_ref[:, hs]) + _dot(em.astype(BF16), vm)) / l
        oa_ref[:, hs] = _diff_finish(o, lam, g, lam_init).astype(BF16)

    org = jnp.bitwise_and(lax.broadcasted_iota(jnp.int32, (1, NA_ROWS * GRID_W), 1), GRID_W - 1) < NA_COLS
    org_bias = jnp.where(org, 0.0, NEG)
    for pr in range(B_HEADS // 2):
        hs = slice(pr * LANES, (pr + 1) * LANES)
        q2 = _split_halves(pm_ref[:, col(QB_BLK, pr)])
        km = _pad_rows(pm_ref[:, col(KB_BLK, pr)], LANES)
        vm = _pad_rows(pm_ref[:, col(VB_BLK, pr)], LANES)
        s = _dot_t(q2, kb_ref[:, hs]) + org_bias
        sm = _dot_t(q2, km) + pad_bias
        m = jnp.maximum(jnp.max(s, axis=-1, keepdims=True), jnp.max(sm, axis=-1, keepdims=True))
        e = jnp.exp2(s - m)
        em = jnp.exp2(sm - m)
        l = jnp.sum(e, axis=-1, keepdims=True) + jnp.sum(em, axis=-1, keepdims=True)
        o = (_dot(e.astype(BF16), vb_ref[:, hs]) + _dot(em.astype(BF16), vm)) / l
        ob_ref[:, hs] = jnp.where(lo, o[:N_META], o[N_META:]).astype(BF16)

    kc_base = KC_BLK128 * LANES
    vc_base = VC_BLK128 * LANES
    km = _pad_rows(pm_ref[:, kc_base:kc_base + LANES], LANES)
    vm = _pad_rows(pm_ref[:, vc_base:vc_base + LANES], LANES)
    npair = C_HEADS // 2
    for pr in range(npair):
        hs = slice(pr * LANES, (pr + 1) * LANES)
        q2 = _split_halves(pm_ref[:, col(QC_BLK, pr)])
        s = _dot_t(q2, kc_ref[...]) + jnp.concatenate([cbr_ref[pr], cbr_ref[npair + pr]], axis=0)
        sm = _dot_t(q2, km) + jnp.concatenate([cbm_ref[pr], cbm_ref[npair + pr]], axis=0)
        sink = jnp.concatenate([jnp.full((N_META, 1), sink_ref[pr], F32),
                                jnp.full((N_META, 1), sink_ref[npair + pr], F32)], axis=0)
        m = jnp.maximum(jnp.maximum(jnp.max(s, axis=-1, keepdims=True), jnp.max(sm, axis=-1, keepdims=True)), sink)
        e = jnp.exp2(s - m)
        em = jnp.exp2(sm - m)
        l = jnp.sum(e, axis=-1, keepdims=True) + jnp.sum(em, axis=-1, keepdims=True) + jnp.exp2(sink - m)
        o = (_dot(e.astype(BF16), vc_ref[...]) + _dot(em.astype(BF16), vm)) / l
        oc_ref[:, hs] = jnp.where(lo, o[:N_META], o[N_META:]).astype(BF16)


def _meta_queries(proj_r, proj_m, sink, lvecs, g_sub, abr, abm, cbr, cbm, lam_init, nb):
    lspec = _const_spec((1, HEAD_DIM), 1)
    org_rows = NA_ROWS * GRID_W
    out = jax.ShapeDtypeStruct((nb * N_META, BRANCH_W), BF16)
    out_spec = pl.BlockSpec((N_META, BRANCH_W), lambda b: (b, 0))
    return pl.pallas_call(
        functools.partial(_meta_kernel, lam_init=lam_init),
        grid=(nb,),
        in_specs=[
            pl.BlockSpec(memory_space=pltpu.SMEM),
            lspec, lspec, lspec, lspec,
            _const_spec((1, LANES), 1),
            pl.BlockSpec((N_META, IN_COLS), lambda b: (b, 0)),
            pl.BlockSpec((SEQ, 512), lambda b: (b, KA_BLK)),
            pl.BlockSpec((SEQ, 512), lambda b: (b, VA_BLK)),
            pl.BlockSpec((org_rows, 512), lambda b: (b * (SEQ // org_rows), KB_BLK)),
            pl.BlockSpec((org_rows, 512), lambda b: (b * (SEQ // org_rows), VB_BLK)),
            pl.BlockSpec((WINDOW, LANES), lambda b: (b * (SEQ // WINDOW), KC_BLK128)),
            pl.BlockSpec((WINDOW, LANES), lambda b: (b * (SEQ // WINDOW), VC_BLK128)),
            _const_spec((A_HEADS, N_META, SEQ), 1),
            _const_spec((A_HEADS, N_META, LANES), 1),
            _const_spec((C_HEADS, N_META, WINDOW), 1),
            _const_spec((C_HEADS, N_META, LANES), 1),
        ],
        out_specs=[out_spec, out_spec, out_spec],
        out_shape=[out, out, out],
        compiler_params=_cparams(1),
        name="meta_queries",
    )(sink, *lvecs, g_sub, proj_m, proj_r, proj_r, proj_r, proj_r, proj_r, proj_r, abr, abm, cbr, cbm)


def _t5_bucket_np(rel):
    nb = T5_BUCKETS // 2
    max_exact = nb // 2
    rel = np.asarray(rel, np.int64)
    n = np.abs(rel)
    n2 = np.maximum(n, 1) ** 2
    large = np.minimum(np.floor(np.log2(n2.astype(np.float64))).astype(np.int64) + 2, nb - 1)
    return (np.where(rel > 0, nb, 0) + np.where(n < max_exact, n, large)).astype(np.int32)


def _t5_vals(table, rel):
    idx = _t5_bucket_np(rel)
    v = jnp.take(table.astype(F32), jnp.asarray(idx.reshape(-1)), axis=0)
    return jnp.moveaxis(v, -1, 0).reshape((table.shape[1],) + idx.shape)


def _t5_dense(table, rel):
    nb = T5_BUCKETS // 2
    max_exact = nb // 2
    n = jnp.abs(rel)
    steps = [int(math.ceil(math.sqrt(2.0 ** k))) for k in range(7, 7 + nb - 1 - max_exact)]
    large = max_exact + sum((n >= t).astype(jnp.int32) for t in steps)
    bucket = jnp.where(rel > 0, nb, 0) + jnp.where(n < max_exact, n, large)
    table = table.astype(F32)
    out = jnp.zeros((table.shape[1],) + rel.shape, F32)
    for b in range(T5_BUCKETS):
        out = jnp.where((bucket == b)[None], table[b].reshape((-1,) + (1,) * rel.ndim), out)
    return out


def _toeplitz(w, n, m):
    p = n + m - 1
    wp = jnp.concatenate([w, jnp.zeros(w.shape[:-1] + (1,), w.dtype)], axis=-1)
    flat = jnp.tile(wp, (1,) * (w.ndim - 1) + (n,))[..., :n * p]
    return flat.reshape(w.shape[:-1] + (n, p))[..., n - 1:n - 1 + m]


def _pad_lanes_neg(x):
    pad = jnp.full(x.shape[:-1] + (LANES - x.shape[-1],), NEG, F32)
    return jnp.concatenate([x, pad], axis=-1)


def _t5_tables(t5_table):
    table = t5_table.astype(F32) * LOG2E
    ta = table[:, :A_HEADS]
    tc = table[:, A_HEADS:]
    out = {}
    rel = (lax.broadcasted_iota(jnp.int32, (BIAS_TALL, TQ), 0) - 3 * TQ
           - lax.broadcasted_iota(jnp.int32, (BIAS_TALL, TQ), 1))
    out["a_tall"] = _t5_dense(ta, rel)
    out["a_far"] = _t5_vals(ta, np.array([-(SEQ + N_META), SEQ + N_META])).T
    rel_t = np.arange(N_META)[:, None] - (N_META + np.arange(TQ))[None, :]
    far_t = np.full((1, 1), -(SEQ + N_META))

    def meta_key_bias_t(tab):
        rest = jnp.broadcast_to(_t5_vals(tab, far_t), (tab.shape[1], N_META, SEQ - TQ))
        return jnp.concatenate([_t5_vals(tab, rel_t), rest], axis=-1)

    out["a_biasm_t"] = meta_key_bias_t(ta)
    out["c_biasm_t"] = meta_key_bias_t(tc)
    offs = np.array([0, -WINDOW, -(GQA_KW - TQ)])
    shape = (3, GQA_KW, TQ)
    relb = (lax.broadcasted_iota(jnp.int32, shape, 1) - lax.broadcasted_iota(jnp.int32, shape, 2)
            + jnp.asarray(offs, jnp.int32)[:, None, None])
    band = jnp.where((jnp.abs(relb) <= WINDOW)[None], _t5_dense(tc, relb), NEG)
    out["c_band_t"] = jnp.swapaxes(band, 0, 1)
    mq = np.arange(N_META)[:, None]
    near_k = N_META + np.arange(TK)[None, :] - mq
    far_k = np.full((1, 1), SEQ + N_META)
    first = _t5_vals(ta, near_k)
    rest = jnp.broadcast_to(_t5_vals(ta, far_k), (A_HEADS, N_META, SEQ - TK))
    out["a_mq_real"] = jnp.concatenate([first, rest], axis=-1)
    out["a_mq_meta"] = _pad_lanes_neg(_t5_vals(ta, np.arange(N_META)[None, :] - mq))
    rel0 = N_META + np.arange(WINDOW)[None, :] - mq
    out["c_mq_real"] = jnp.where(jnp.asarray(rel0 <= WINDOW)[None], _t5_vals(tc, rel0), NEG)
    out["c_mq_meta"] = _pad_lanes_neg(_t5_vals(tc, np.arange(N_META)[None, :] - mq))
    return out


def _nat_bias_t(rpb):
    n_rows = SEQ // GRID_W
    w = jnp.pad(rpb.astype(F32) * LOG2E, ((0, 0), (0, 0), (GRID_W - NA_COLS, GRID_W - NA_COLS)))
    tiles = _toeplitz(w, GRID_W, GRID_W)
    cols = np.arange(GRID_W)
    cstart = np.clip(cols - NA_COLS // 2, 0, GRID_W - NA_COLS)
    col_ok = (cols[None, :] >= cstart[:, None]) & (cols[None, :] < cstart[:, None] + NA_COLS)
    tiles = jnp.where(jnp.asarray(col_ok)[None, None], tiles, NEG)
    neg_tile = jnp.full((B_HEADS, 1, GRID_W, GRID_W), NEG, F32)
    tiles = jnp.concatenate([tiles, neg_tile], axis=1)
    n_blk = n_rows // NAT_R
    idx = np.zeros((3, NAT_R, NAT_W), np.int32)
    for v, blk in enumerate((0, 1, n_blk - 1)):
        r0 = NAT_R * blk
        w0 = int(np.clip(r0 - NA_ROWS // 2, 0, n_rows - NAT_W))
        for qr in range(NAT_R):
            r = r0 + qr
            rs = int(np.clip(r - NA_ROWS // 2, 0, n_rows - NA_ROWS))
            for kr in range(NAT_W):
                krow = w0 + kr
                idx[v, qr, kr] = krow - r + NA_ROWS - 1 if rs <= krow < rs + NA_ROWS else 2 * NA_ROWS - 1
    g = jnp.take(tiles, jnp.asarray(idx.reshape(-1)), axis=1)
    g = g.reshape(B_HEADS, 3, NAT_R, NAT_W, GRID_W, GRID_W)
    g = jnp.transpose(g, (1, 0, 3, 5, 2, 4))
    return g.reshape(3, B_HEADS, NAT_W * GRID_W, NAT_R * GRID_W)


def _prep_w_in(w):
    col = np.arange(IN_COLS)
    is_q = (col < 512) | ((col >= 1536) & (col < 2048)) | ((col >= 3072) & (col < 3584))
    w = w * jnp.asarray(np.where(is_q, SCALE * LOG2E, 1.0), F32)[None, :]
    qc = w[:, 3072:3584].reshape(D_MODEL, C_HEADS, HEAD_DIM)[:, np.asarray(C_PERM)].reshape(D_MODEL, 512)
    w = jnp.concatenate([w[:, :3072], qc, w[:, 3584:]], axis=1)
    return w.astype(BF16)


def _meta_vt(p_m, lo, groups, nb):
    v = p_m[:, lo:lo + groups * LANES].reshape(nb, N_META, groups, LANES)
    v = jnp.transpose(v, (0, 2, 3, 1))
    ones = jnp.ones((nb, groups, VT_ROWS - LANES, N_META), BF16)
    return jnp.concatenate([v, ones], axis=2).reshape(nb, groups * VT_ROWS, N_META)


def kernel(x, meta_tokens, t5_table, norm_ffn1, w_ffn1_in, w_ffn1_out, norm_mix, w_in, lambda_q1, lambda_k1,
           lambda_q2, lambda_k2, subln_gain, natten_rpb, sink_logits, w_branch, w_gate, w_out, norm_ffn2,
           w_ffn2_in, w_ffn2_out, final_norm):
    nb, seq, d = x.shape
    assert (seq, d) == (SEQ, D_MODEL)
    depth = norm_ffn1.shape[0]
    h_r = x.reshape(nb * SEQ, D_MODEL)
    h_m = jnp.broadcast_to(meta_tokens[None].astype(x.dtype), (nb, N_META, D_MODEL)).reshape(nb * N_META, D_MODEL)
    t5 = _t5_tables(t5_table)
    perm = np.asarray(C_PERM)
    out = None
    for l in range(depth):
        lam_init = 0.8 - 0.6 * math.exp(-0.3 * l)
        w1i, w1o = w_ffn1_in[l].astype(BF16), w_ffn1_out[l].astype(BF16)
        w2i, w2o = w_ffn2_in[l].astype(BF16), w_ffn2_out[l].astype(BF16)
        wi = _prep_w_in(w_in[l])
        wvt = jnp.concatenate([wi[:, 1024:1536], wi[:, 2560:3072], wi[:, 3712:3840]], axis=1).T
        wg = w_gate[l].astype(BF16)
        wb2 = w_branch[l, 2].reshape(C_HEADS, HEAD_DIM, D_MODEL)[perm].reshape(BRANCH_W, D_MODEL)
        wb = jnp.stack([w_branch[l, 0], w_branch[l, 1], wb2]).astype(BF16)
        wo = w_out[l].astype(BF16)
        lvecs = [v[l].reshape(1, HEAD_DIM).astype(F32) for v in (lambda_q1, lambda_k1, lambda_q2, lambda_k2)]
        g_sub = subln_gain[l].reshape(1, LANES).astype(F32)
        sink = sink_logits[l].astype(F32)[perm] * LOG2E
        nat_bias = _nat_bias_t(natten_rpb[l])

        h_r = _ffn(h_r, norm_ffn1[l], w1i, w1o)
        h_m = _ffn(h_m, norm_ffn1[l], w1i, w1o)

        p_r, vta_r, vtb_r, vtc_r = _inproj(h_r, norm_mix[l], wi, wvt)
        p_m = _inproj(h_m, norm_mix[l], wi)
        vta_m = _meta_vt(p_m, 1024, VT_GROUPS[0], nb)
        vtb_m = _meta_vt(p_m, 2560, VT_GROUPS[1], nb)
        vtc_m = _meta_vt(p_m, 3712, VT_GROUPS[2], nb)
        ya_r = _diff_real(p_r, vta_r, p_m, vta_m, lvecs, g_sub.reshape(LANES, 1), t5["a_far"], t5["a_tall"],
                          t5["a_biasm_t"], lam_init, nb)
        yb_r = _nat_real(p_r, vtb_r, p_m, vtb_m, nat_bias, nb)
        yc_r = _gqa_real(p_r, vtc_r, p_m, vtc_m, sink, t5["c_band_t"], t5["c_biasm_t"], nb)
        ya_m, yb_m, yc_m = _meta_queries(p_r, p_m, sink, lvecs, g_sub, t5["a_mq_real"], t5["a_mq_meta"],
                                         t5["c_mq_real"], t5["c_mq_meta"], lam_init, nb)
        h_r = _merge(h_r, norm_mix[l], ya_r, yb_r, yc_r, wg, wb, wo)
        h_m = _merge(h_m, norm_mix[l], ya_m, yb_m, yc_m, wg, wb, wo)

        last = l == depth - 1
        h_r = _ffn(h_r, norm_ffn2[l], w2i, w2o, final_gain=final_norm if last else None)
        h_m = _ffn(h_m, norm_ffn2[l], w2i, w2o)
        out = h_r
    return out.reshape(nb, SEQ, D_MODEL)
```

```python
import functools
import math

import numpy as np
import jax
import jax.numpy as jnp
from jax import lax
from jax.experimental import pallas as pl
from jax.experimental.pallas import tpu as pltpu

F32 = jnp.float32
BF16 = jnp.bfloat16

D_MODEL = 1024
SEQ = 4096
N_META = 16
GRID_W = 64
HEAD_DIM = 64
LANES = 128
A_HEADS = 4
B_HEADS = 8
C_HEADS = 8
C_KV_HEADS = 2
NA_ROWS = 8
NA_COLS = 16
WINDOW = 128
T5_BUCKETS = 32
D_FF = 2816
BRANCH_W = 512
IN_COLS = 3840
EPS = 1e-6
NEG = -1e30
SCALE = HEAD_DIM ** -0.5
LOG2E = math.log2(math.e)

QA_BLK, KA_BLK, VA_BLK, QB_BLK, KB_BLK, VB_BLK, QC_BLK = 0, 1, 2, 3, 4, 5, 6
KC_BLK128, VC_BLK128 = 28, 29
C_PERM = (0, 4, 1, 5, 2, 6, 3, 7)

TM_REAL = 1024
FF_CHUNK = 256
PROJ_CHUNK = 768
TQ = 256
TK = 256
TKD = 256
LOOKAHEAD = 2
DIFF_QB = 2
LOCAL_QB = 4
NAT_LOOKAHEAD = 3
VT_ROWS = 144
VT_GROUPS = (A_HEADS, B_HEADS // 2, 1)
VT_CHUNK = (TKD, 256, 128)
BIAS_TALL = 6 * TQ + TKD - TQ
NAT_R = 4
NAT_W = 12
GQA_KW = 512
VMEM_LIMIT = 56 * 1024 * 1024


def _cparams(n_axes, **kw):
    return pltpu.CompilerParams(dimension_semantics=("parallel",) * n_axes,
                                vmem_limit_bytes=VMEM_LIMIT, **kw)


def _const_spec(shape, n_grid):
    zeros = (0,) * len(shape)
    if n_grid == 1:
        return pl.BlockSpec(shape, lambda i: zeros, pipeline_mode=pl.Buffered(1))
    return pl.BlockSpec(shape, lambda i, j: zeros, pipeline_mode=pl.Buffered(1))


def _rms(x, g):
    return x * lax.rsqrt(jnp.mean(x * x, axis=-1, keepdims=True) + EPS) * g


def _dot(a, b):
    return jnp.dot(a, b, preferred_element_type=F32)


def _dot_t(a, b):
    return lax.dot_general(a, b, (((1,), (1,)), ((), ())), preferred_element_type=F32)


def _ffn_kernel(h_ref, g_ref, win_ref, wout_ref, *rest, final):
    if final:
        fg_ref, o_ref, act_ref = rest
    else:
        o_ref, act_ref = rest
    x = h_ref[...]
    xn = _rms(x, g_ref[...]).astype(BF16)
    for c in range(D_FF // FF_CHUNK):
        lo = c * FF_CHUNK
        gg = _dot(xn, win_ref[:, lo:lo + FF_CHUNK])
        uu = _dot(xn, win_ref[:, D_FF + lo:D_FF + lo + FF_CHUNK])
        act_ref[:, lo:lo + FF_CHUNK] = (gg * jax.nn.sigmoid(gg) * uu).astype(BF16)
    hn = x + 0.5 * _dot(act_ref[...], wout_ref[...])
    if final:
        hn = _rms(hn, fg_ref[...])
    o_ref[...] = hn


def _ffn(h, gain, w_in, w_out, final_gain=None):
    rows = h.shape[0]
    tm = min(TM_REAL, rows)
    final = final_gain is not None
    in_specs = [
        pl.BlockSpec((tm, D_MODEL), lambda i: (i, 0)),
        _const_spec((1, D_MODEL), 1),
        _const_spec((D_MODEL, 2 * D_FF), 1),
        _const_spec((D_FF, D_MODEL), 1),
    ]
    args = [h, gain.reshape(1, D_MODEL), w_in, w_out]
    if final:
        in_specs.append(_const_spec((1, D_MODEL), 1))
        args.append(final_gain.reshape(1, D_MODEL))
    return pl.pallas_call(
        functools.partial(_ffn_kernel, final=final),
        grid=(rows // tm,),
        in_specs=in_specs,
        out_specs=pl.BlockSpec((tm, D_MODEL), lambda i: (i, 0)),
        out_shape=jax.ShapeDtypeStruct((rows, D_MODEL), F32),
        scratch_shapes=[pltpu.VMEM((tm, D_FF), BF16)],
        compiler_params=_cparams(1),
        name="ffn",
    )(*args)


def _inproj_kernel(h_ref, g_ref, w_ref, *rest, transposed):
    if transposed:
        wvt_ref, o_ref, *vt_refs = rest
    else:
        (o_ref,) = rest
    xn = _rms(h_ref[...], g_ref[...]).astype(BF16)
    for c in range(IN_COLS // PROJ_CHUNK):
        lo = c * PROJ_CHUNK
        o_ref[:, lo:lo + PROJ_CHUNK] = _dot(xn, w_ref[:, lo:lo + PROJ_CHUNK]).astype(BF16)
    if transposed:
        tm = xn.shape[0]
        vt = _dot_t(wvt_ref[...], xn).astype(BF16)
        base = 0
        for vt_ref, groups, chunk in zip(vt_refs, VT_GROUPS, VT_CHUNK):
            for c in range(tm // chunk):
                cols = slice(c * chunk, (c + 1) * chunk)
                for g in range(groups):
                    vt_ref[c, g * VT_ROWS:g * VT_ROWS + LANES, :] = vt[base + g * LANES:base + (g + 1) * LANES, cols]
                    vt_ref[c, g * VT_ROWS + LANES:(g + 1) * VT_ROWS, :] = jnp.ones((VT_ROWS - LANES, chunk), BF16)
            base += groups * LANES


def _inproj(h, gain, w_in, wvt=None):
    rows = h.shape[0]
    tm = min(TM_REAL, rows)
    transposed = wvt is not None
    in_specs = [
        pl.BlockSpec((tm, D_MODEL), lambda i: (i, 0)),
        _const_spec((1, D_MODEL), 1),
        _const_spec((D_MODEL, IN_COLS), 1),
    ]
    args = [h, gain.reshape(1, D_MODEL), w_in]
    out_specs = [pl.BlockSpec((tm, IN_COLS), lambda i: (i, 0))]
    out_shape = [jax.ShapeDtypeStruct((rows, IN_COLS), BF16)]
    if transposed:
        in_specs.append(_const_spec(wvt.shape, 1))
        args.append(wvt)
        for groups, chunk in zip(VT_GROUPS, VT_CHUNK):
            out_specs.append(pl.BlockSpec((tm // chunk, groups * VT_ROWS, chunk), lambda i: (i, 0, 0)))
            out_shape.append(jax.ShapeDtypeStruct((rows // chunk, groups * VT_ROWS, chunk), BF16))
    res = pl.pallas_call(
        functools.partial(_inproj_kernel, transposed=transposed),
        grid=(rows // tm,),
        in_specs=in_specs,
        out_specs=out_specs,
        out_shape=out_shape,
        compiler_params=_cparams(1),
        name="inproj",
    )(*args)
    return res if transposed else res[0]


def _merge_kernel(h_ref, g_ref, ya_ref, yb_ref, yc_ref, wg_ref, wb_ref, wo_ref, o_ref):
    x = h_ref[...]
    xn = _rms(x, g_ref[...]).astype(BF16)
    merged = None
    for i, y_ref in enumerate((ya_ref, yb_ref, yc_ref)):
        term = jax.nn.sigmoid(_dot(xn, wg_ref[i])) * _dot(y_ref[...], wb_ref[i])
        merged = term if merged is None else merged + term
    o_ref[...] = x + _dot(merged.astype(BF16), wo_ref[...])


def _merge(h, gain, ya, yb, yc, w_gate, w_branch, w_out):
    rows = h.shape[0]
    tm = min(TM_REAL, rows)
    row_spec = lambda w: pl.BlockSpec((tm, w), lambda i: (i, 0))
    return pl.pallas_call(
        _merge_kernel,
        grid=(rows // tm,),
        in_specs=[
            row_spec(D_MODEL),
            _const_spec((1, D_MODEL), 1),
            row_spec(BRANCH_W), row_spec(BRANCH_W), row_spec(BRANCH_W),
            _const_spec((3, D_MODEL, D_MODEL), 1),
            _const_spec((3, BRANCH_W, D_MODEL), 1),
            _const_spec((D_MODEL, D_MODEL), 1),
        ],
        out_specs=row_spec(D_MODEL),
        out_shape=jax.ShapeDtypeStruct((rows, D_MODEL), F32),
        compiler_params=_cparams(1),
        name="merge",
    )(h, gain.reshape(1, D_MODEL), ya, yb, yc, w_gate, w_branch, w_out)


def _lane_lo(rows):
    return lax.broadcasted_iota(jnp.int32, (rows, LANES), 1) < HEAD_DIM


def _split_halves(q):
    lo = _lane_lo(q.shape[0])
    zero = jnp.zeros_like(q)
    return jnp.concatenate([jnp.where(lo, q, zero), jnp.where(lo, zero, q)], axis=0)


def _pad_rows(x, rows):
    return jnp.concatenate([x, jnp.zeros((rows - x.shape[0], x.shape[1]), x.dtype)], axis=0)


def _twice(b):
    return jnp.concatenate([b, b], axis=0)


def _side_by_side(a, b):
    return jnp.concatenate([a, b], axis=1)


def _lam(lq1_ref, lk1_ref, lq2_ref, lk2_ref, lam_init):
    s1 = jnp.sum(lq1_ref[...] * lk1_ref[...], axis=-1, keepdims=True)
    s2 = jnp.sum(lq2_ref[...] * lk2_ref[...], axis=-1, keepdims=True)
    return jnp.exp(s1) - jnp.exp(s2) + lam_init


def _diff_finish(o, lam, g, lam_init):
    m = o.shape[0] // 2
    d = o[:m] - lam * o[m:]
    return _rms(d, g) * (1.0 - lam_init)


def _pipeline(n, issue, consume, scr):
    ahead = len(scr) - 1
    pending = {}
    for t in range(min(ahead, n)):
        pending[t] = issue(t, scr[t % len(scr)])
    for t in range(n):
        if t + ahead < n:
            pending[t + ahead] = issue(t + ahead, scr[(t + ahead) % len(scr)])
        consume(t, scr[t % len(scr)], pending.pop(t))


def _pair_out(o):
    top = lax.broadcasted_iota(jnp.int32, (LANES, TQ), 0) < HEAD_DIM
    return jnp.where(top, o[:, :TQ], o[:, TQ:]).T


def _diff_real_kernel(far_ref, lq1_ref, lk1_ref, lq2_ref, lk2_ref, g_ref, q_ref, k_ref, vt_ref, km_ref, vmt_ref,
                      bias_ref, biasm_ref, o_ref, *scr, lam_init):
    lam = _lam(lq1_ref, lk1_ref, lq2_ref, lk2_ref, lam_init)
    g = g_ref[...]
    nk = k_ref.shape[0] // TKD
    ratio = TKD // TQ
    near_slots = 2 if ratio > 1 else 3

    def head_setup(qb, h):
        hs = slice(h * LANES, (h + 1) * LANES)
        q12 = _split_halves(q_ref[qb * TQ:(qb + 1) * TQ, hs])
        bm = biasm_ref[h, :, qb * TQ:(qb + 1) * TQ]
        s = _dot_t(km_ref[:, hs], q12) + _side_by_side(bm, bm)
        m0 = jnp.max(s, axis=0, keepdims=True)
        p = jnp.exp2(s - m0).astype(BF16)
        acc0 = _dot(vmt_ref[0, h * VT_ROWS:(h + 1) * VT_ROWS, :], p)
        return q12, m0, acc0

    def scores(qb, h, u, q12, dst_ref):
        i = DIFF_QB * pl.program_id(1) + qb
        jc = (i + ratio - 1) // ratio
        ju = jc - 1 + u
        j = jnp.where(ju >= nk, ju - nk, jnp.where(ju < 0, ju + nk, ju))
        kt = k_ref[pl.ds(pl.multiple_of(j * TKD, TKD), TKD), h * LANES:(h + 1) * LANES]
        s = _dot_t(kt, q12)
        if u < near_slots:
            e = jnp.clip(ratio * j - i, -3, 2) + 3
            bt = bias_ref[h, pl.ds(pl.multiple_of(e * TQ, TQ), TKD), :]
            s = s + _side_by_side(bt, bt)
            c = None
        else:
            c = jnp.where(ju >= nk, far_ref[0, h], far_ref[1, h])
        dst_ref[...] = s
        smax = jnp.max(s, axis=0, keepdims=True)
        return j, c, smax if c is None else smax + c

    def fold(h, src_ref, issued, m, acc):
        j, c, smax = issued
        m_new = jnp.maximum(m, smax)
        alpha = jnp.exp2(m - m_new)
        p = jnp.exp2(src_ref[...] - (m_new if c is None else m_new - c)).astype(BF16)
        return m_new, alpha * acc + _dot(vt_ref[j, h * VT_ROWS:(h + 1) * VT_ROWS, :], p)

    def finish(qb, h, acc):
        o = acc[:LANES] / acc[LANES:LANES + 1]
        d = o[:, :TQ] - lam * o[:, TQ:]
        y = d * lax.rsqrt(jnp.mean(d * d, axis=0, keepdims=True) + EPS) * g * (1.0 - lam_init)
        o_ref[qb * TQ:(qb + 1) * TQ, h * LANES:(h + 1) * LANES] = y.T.astype(BF16)

    tiles = [(qb, h, u) for qb in range(DIFF_QB) for h in range(A_HEADS) for u in range(nk)]
    setup = {}
    issued = {}

    def issue(t):
        qb, h, u = tiles[t]
        if (qb, h) not in setup:
            setup[qb, h] = head_setup(qb, h)
        issued[t] = scores(qb, h, u, setup[qb, h][0], scr[t % len(scr)])

    for t in range(LOOKAHEAD):
        issue(t)
    m = acc = None
    for t, (qb, h, u) in enumerate(tiles):
        if t + LOOKAHEAD < len(tiles):
            issue(t + LOOKAHEAD)
        if u == 0:
            _, m, acc = setup[qb, h]
        m, acc = fold(h, scr[t % len(scr)], issued.pop(t), m, acc)
        if u == nk - 1:
            finish(qb, h, acc)


def _diff_real(proj_r, vt, proj_m, vmt, lvecs, g_col, bias_far, bias_tall, biasm_t, lam_init, nb):
    nq = SEQ // (DIFF_QB * TQ)
    tq = DIFF_QB * TQ
    lspec = _const_spec((1, HEAD_DIM), 2)
    return pl.pallas_call(
        functools.partial(_diff_real_kernel, lam_init=lam_init),
        grid=(nb, nq),
        in_specs=[
            pl.BlockSpec(memory_space=pltpu.SMEM),
            lspec, lspec, lspec, lspec,
            _const_spec((LANES, 1), 2),
            pl.BlockSpec((tq, 512), lambda b, i: (b * nq + i, QA_BLK)),
            pl.BlockSpec((SEQ, 512), lambda b, i: (b, KA_BLK)),
            pl.BlockSpec((SEQ // TKD, A_HEADS * VT_ROWS, TKD), lambda b, i: (b, 0, 0)),
            pl.BlockSpec((N_META, 512), lambda b, i: (b, KA_BLK)),
            pl.BlockSpec((1, A_HEADS * VT_ROWS, N_META), lambda b, i: (b, 0, 0)),
            _const_spec((A_HEADS, BIAS_TALL, TQ), 2),
            pl.BlockSpec((A_HEADS, N_META, tq), lambda b, i: (0, 0, i)),
        ],
        out_specs=pl.BlockSpec((tq, 512), lambda b, i: (b * nq + i, 0)),
        out_shape=jax.ShapeDtypeStruct((nb * SEQ, BRANCH_W), BF16),
        scratch_shapes=[pltpu.VMEM((TKD, 2 * TQ), F32)] * (LOOKAHEAD + 1),
        compiler_params=_cparams(2),
        name="diff_real",
    )(bias_far, *lvecs, g_col, proj_r, proj_r, vt, proj_m, vmt, bias_tall, biasm_t)


def _nat_real_kernel(q_ref, k_ref, vt_ref, km_ref, vmt_ref, bias_ref, o_ref, *scr):
    chunk = vt_ref.shape[2]
    n_chunks = NAT_W * GRID_W // chunk
    n_blk = vt_ref.shape[0] * chunk // TQ
    npair = B_HEADS // 2

    def block(qb):
        blk = LOCAL_QB * pl.program_id(1) + qb
        c0 = jnp.clip(blk - 1, 0, vt_ref.shape[0] - n_chunks)
        return c0, jnp.where(blk == 0, 0, jnp.where(blk == n_blk - 1, 2, 1))

    ctx = [block(qb) for qb in range(LOCAL_QB)]

    def scores(t, dst_ref):
        qb, pr = divmod(t, npair)
        c0, variant = ctx[qb]
        hs = slice(pr * LANES, (pr + 1) * LANES)
        q2 = _split_halves(q_ref[qb * TQ:(qb + 1) * TQ, hs])
        kw = k_ref[pl.ds(pl.multiple_of(c0 * chunk, chunk), n_chunks * chunk), hs]
        s = _dot_t(kw, q2) + _side_by_side(bias_ref[variant, 2 * pr], bias_ref[variant, 2 * pr + 1])
        dst_ref[...] = s
        sm = _dot_t(km_ref[:, hs], q2)
        return sm, jnp.maximum(jnp.max(s, axis=0, keepdims=True), jnp.max(sm, axis=0, keepdims=True))

    def finish(t, src_ref, pending):
        qb, pr = divmod(t, npair)
        c0, _ = ctx[qb]
        sm, m = pending
        rows = slice(pr * VT_ROWS, (pr + 1) * VT_ROWS)
        acc = _dot(vmt_ref[0, rows, :], jnp.exp2(sm - m).astype(BF16))
        for c in range(n_chunks):
            p = jnp.exp2(src_ref[c * chunk:(c + 1) * chunk, :] - m).astype(BF16)
            acc = acc + _dot(vt_ref[c0 + c, rows, :], p)
        out = _pair_out(acc[:LANES] / acc[LANES:LANES + 1]).astype(BF16)
        o_ref[qb * TQ:(qb + 1) * TQ, pr * LANES:(pr + 1) * LANES] = out

    _pipeline(LOCAL_QB * npair, scores, finish, scr)


def _nat_real(proj_r, vt, proj_m, vmt, bias_t, nb):
    tq = LOCAL_QB * TQ
    nq = SEQ // tq
    nkeys = NAT_W * GRID_W
    chunk = VT_CHUNK[1]
    return pl.pallas_call(
        _nat_real_kernel,
        grid=(nb, nq),
        in_specs=[
            pl.BlockSpec((tq, 512), lambda b, i: (b * nq + i, QB_BLK)),
            pl.BlockSpec((SEQ, 512), lambda b, i: (b, KB_BLK)),
            pl.BlockSpec((SEQ // chunk, VT_GROUPS[1] * VT_ROWS, chunk), lambda b, i: (b, 0, 0)),
            pl.BlockSpec((N_META, 512), lambda b, i: (b, KB_BLK)),
            pl.BlockSpec((1, VT_GROUPS[1] * VT_ROWS, N_META), lambda b, i: (b, 0, 0)),
            _const_spec((3, B_HEADS, nkeys, TQ), 2),
        ],
        out_specs=pl.BlockSpec((tq, 512), lambda b, i: (b * nq + i, 0)),
        out_shape=jax.ShapeDtypeStruct((nb * SEQ, BRANCH_W), BF16),
        scratch_shapes=[pltpu.VMEM((nkeys, 2 * TQ), F32)] * (NAT_LOOKAHEAD + 1),
        compiler_params=_cparams(2),
        name="nat_real",
    )(proj_r, proj_r, vt, proj_m, vmt, bias_t)


def _gqa_real_kernel(sink_ref, q_ref, k_ref, vt_ref, km_ref, vmt_ref, band_ref, mb_ref, o_ref, *scr):
    chunk = vt_ref.shape[2]
    n_chunks = GQA_KW // chunk
    n_blk = vt_ref.shape[0] * chunk // TQ
    npair = C_HEADS // 2

    def block(qb):
        t = LOCAL_QB * pl.program_id(1) + qb
        c0 = jnp.clip((TQ // chunk) * t - WINDOW // chunk, 0, vt_ref.shape[0] - n_chunks)
        return c0, jnp.where(t == 0, 0, jnp.where(t == n_blk - 1, 2, 1))

    ctx = [block(qb) for qb in range(LOCAL_QB)]

    def scores(t, dst_ref):
        qb, pr = divmod(t, npair)
        c0, variant = ctx[qb]
        cols = slice(qb * TQ, (qb + 1) * TQ)
        q2 = _split_halves(q_ref[cols, pr * LANES:(pr + 1) * LANES])
        kw = k_ref[pl.ds(pl.multiple_of(c0 * chunk, chunk), GQA_KW), :]
        s = _dot_t(kw, q2) + _side_by_side(band_ref[variant, pr], band_ref[variant, npair + pr])
        dst_ref[...] = s
        sm = _dot_t(km_ref[...], q2) + _side_by_side(mb_ref[pr, :, cols], mb_ref[npair + pr, :, cols])
        sink = _side_by_side(jnp.full((1, TQ), sink_ref[pr], F32), jnp.full((1, TQ), sink_ref[npair + pr], F32))
        m = jnp.maximum(jnp.maximum(jnp.max(s, axis=0, keepdims=True), jnp.max(sm, axis=0, keepdims=True)), sink)
        return sm, sink, m

    def finish(t, src_ref, pending):
        qb, pr = divmod(t, npair)
        c0, _ = ctx[qb]
        sm, sink, m = pending
        acc = _dot(vmt_ref[0], jnp.exp2(sm - m).astype(BF16))
        for c in range(n_chunks // 2):
            vt2 = _side_by_side(vt_ref[c0 + 2 * c], vt_ref[c0 + 2 * c + 1])
            p = jnp.exp2(src_ref[2 * c * chunk:2 * (c + 1) * chunk, :] - m).astype(BF16)
            acc = acc + _dot(vt2, p)
        l = acc[LANES:LANES + 1] + jnp.exp2(sink - m)
        o_ref[qb * TQ:(qb + 1) * TQ, pr * LANES:(pr + 1) * LANES] = _pair_out(acc[:LANES] / l).astype(BF16)

    _pipeline(LOCAL_QB * npair, scores, finish, scr)


def _gqa_real(proj_r, vt, proj_m, vmt, sink, band_t, mbias_t, nb):
    tq = LOCAL_QB * TQ
    nq = SEQ // tq
    chunk = VT_CHUNK[2]
    return pl.pallas_call(
        _gqa_real_kernel,
        grid=(nb, nq),
        in_specs=[
            pl.BlockSpec(memory_space=pltpu.SMEM),
            pl.BlockSpec((tq, 512), lambda b, i: (b * nq + i, QC_BLK)),
            pl.BlockSpec((SEQ, LANES), lambda b, i: (b, KC_BLK128)),
            pl.BlockSpec((SEQ // chunk, VT_ROWS, chunk), lambda b, i: (b, 0, 0)),
            pl.BlockSpec((N_META, LANES), lambda b, i: (b, KC_BLK128)),
            pl.BlockSpec((1, VT_ROWS, N_META), lambda b, i: (b, 0, 0)),
            _const_spec((3, C_HEADS, GQA_KW, TQ), 2),
            pl.BlockSpec((C_HEADS, N_META, tq), lambda b, i: (0, 0, i)),
        ],
        out_specs=pl.BlockSpec((tq, 512), lambda b, i: (b * nq + i, 0)),
        out_shape=jax.ShapeDtypeStruct((nb * SEQ, BRANCH_W), BF16),
        scratch_shapes=[pltpu.VMEM((GQA_KW, 2 * TQ), F32)] * (LOOKAHEAD + 1),
        compiler_params=_cparams(2),
        name="gqa_real",
    )(sink, proj_r, proj_r, vt, proj_m, vmt, band_t, mbias_t)


def _meta_kernel(sink_ref, lq1_ref, lk1_ref, lq2_ref, lk2_ref, g_ref, pm_ref, ka_ref, va_ref, kb_ref, vb_ref,
                 kc_ref, vc_ref, abr_ref, abm_ref, cbr_ref, cbm_ref, oa_ref, ob_ref, oc_ref, *, lam_init):
    lam = _lam(lq1_ref, lk1_ref, lq2_ref, lk2_ref, lam_init)
    g = g_ref[...]
    lo = _lane_lo(N_META)
    lane = lax.broadcasted_iota(jnp.int32, (1, LANES), 1)
    pad_bias = jnp.where(lane < N_META, 0.0, NEG)

    def col(blk512, sub):
        base = blk512 * 512 + sub * LANES
        return slice(base, base + LANES)

    for h in range(A_HEADS):
        hs = slice(h * LANES, (h + 1) * LANES)
        q12 = _split_halves(pm_ref[:, col(QA_BLK, h)])
        km = _pad_rows(pm_ref[:, col(KA_BLK, h)], LANES)
        vm = _pad_rows(pm_ref[:, col(VA_BLK, h)], LANES)
        s = _dot_t(q12, ka_ref[:, hs]) + _twice(abr_ref[h])
        sm = _dot_t(q12, km) + _twice(abm_ref[h])
        m = jnp.maximum(jnp.max(s, axis=-1, keepdims=True), jnp.max(sm, axis=-1, keepdims=True))
        e = jnp.exp2(s - m)
        em = jnp.exp2(sm - m)
        l = jnp.sum(e, axis=-1, keepdims=True) + jnp.sum(em, axis=-1, keepdims=True)
        o = (_dot(e.astype(BF16), va_ref[:, hs]) + _dot(em.astype(BF16), vm)) / l
        oa_ref[:, hs] = _diff_finish(o, lam, g, lam_init).astype(BF16)

    org = jnp.bitwise_and(lax.broadcasted_iota(jnp.int32, (1, NA_ROWS * GRID_W), 1), GRID_W - 1) < NA_COLS
    org_bias = jnp.where(org, 0.0, NEG)
    for pr in range(B_HEADS // 2):
        hs = slice(pr * LANES, (pr + 1) * LANES)
        q2 = _split_halves(pm_ref[:, col(QB_BLK, pr)])
        km = _pad_rows(pm_ref[:, col(KB_BLK, pr)], LANES)
        vm = _pad_rows(pm_ref[:, col(VB_BLK, pr)], LANES)
        s = _dot_t(q2, kb_ref[:, hs]) + org_bias
        sm = _dot_t(q2, km) + pad_bias
        m = jnp.maximum(jnp.max(s, axis=-1, keepdims=True), jnp.max(sm, axis=-1, keepdims=True))
        e = jnp.exp2(s - m)
        em = jnp.exp2(sm - m)
        l = jnp.sum(e, axis=-1, keepdims=True) + jnp.sum(em, axis=-1, keepdims=True)
        o = (_dot(e.astype(BF16), vb_ref[:, hs]) + _dot(em.astype(BF16), vm)) / l
        ob_ref[:, hs] = jnp.where(lo, o[:N_META], o[N_META:]).astype(BF16)

    kc_base = KC_BLK128 * LANES
    vc_base = VC_BLK128 * LANES
    km = _pad_rows(pm_ref[:, kc_base:kc_base + LANES], LANES)
    vm = _pad_rows(pm_ref[:, vc_base:vc_base + LANES], LANES)
    npair = C_HEADS // 2
    for pr in range(npair):
        hs = slice(pr * LANES, (pr + 1) * LANES)
        q2 = _split_halves(pm_ref[:, col(QC_BLK, pr)])
        s = _dot_t(q2, kc_ref[...]) + jnp.concatenate([cbr_ref[pr], cbr_ref[npair + pr]], axis=0)
        sm = _dot_t(q2, km) + jnp.concatenate([cbm_ref[pr], cbm_ref[npair + pr]], axis=0)
        sink = jnp.concatenate([jnp.full((N_META, 1), sink_ref[pr], F32),
                                jnp.full((N_META, 1), sink_ref[npair + pr], F32)], axis=0)
        m = jnp.maximum(jnp.maximum(jnp.max(s, axis=-1, keepdims=True), jnp.max(sm, axis=-1, keepdims=True)), sink)
        e = jnp.exp2(s - m)
        em = jnp.exp2(sm - m)
        l = jnp.sum(e, axis=-1, keepdims=True) + jnp.sum(em, axis=-1, keepdims=True) + jnp.exp2(sink - m)
        o = (_dot(e.astype(BF16), vc_ref[...]) + _dot(em.astype(BF16), vm)) / l
        oc_ref[:, hs] = jnp.where(lo, o[:N_META], o[N_META:]).astype(BF16)


def _meta_queries(proj_r, proj_m, sink, lvecs, g_sub, abr, abm, cbr, cbm, lam_init, nb):
    lspec = _const_spec((1, HEAD_DIM), 1)
    org_rows = NA_ROWS * GRID_W
    out = jax.ShapeDtypeStruct((nb * N_META, BRANCH_W), BF16)
    out_spec = pl.BlockSpec((N_META, BRANCH_W), lambda b: (b, 0))
    return pl.pallas_call(
        functools.partial(_meta_kernel, lam_init=lam_init),
        grid=(nb,),
        in_specs=[
            pl.BlockSpec(memory_space=pltpu.SMEM),
            lspec, lspec, lspec, lspec,
            _const_spec((1, LANES), 1),
            pl.BlockSpec((N_META, IN_COLS), lambda b: (b, 0)),
            pl.BlockSpec((SEQ, 512), lambda b: (b, KA_BLK)),
            pl.BlockSpec((SEQ, 512), lambda b: (b, VA_BLK)),
            pl.BlockSpec((org_rows, 512), lambda b: (b * (SEQ // org_rows), KB_BLK)),
            pl.BlockSpec((org_rows, 512), lambda b: (b * (SEQ // org_rows), VB_BLK)),
            pl.BlockSpec((WINDOW, LANES), lambda b: (b * (SEQ // WINDOW), KC_BLK128)),
            pl.BlockSpec((WINDOW, LANES), lambda b: (b * (SEQ // WINDOW), VC_BLK128)),
            _const_spec((A_HEADS, N_META, SEQ), 1),
            _const_spec((A_HEADS, N_META, LANES), 1),
            _const_spec((C_HEADS, N_META, WINDOW), 1),
            _const_spec((C_HEADS, N_META, LANES), 1),
        ],
        out_specs=[out_spec, out_spec, out_spec],
        out_shape=[out, out, out],
        compiler_params=_cparams(1),
        name="meta_queries",
    )(sink, *lvecs, g_sub, proj_m, proj_r, proj_r, proj_r, proj_r, proj_r, proj_r, abr, abm, cbr, cbm)


def _t5_bucket_np(rel):
    nb = T5_BUCKETS // 2
    max_exact = nb // 2
    rel = np.asarray(rel, np.int64)
    n = np.abs(rel)
    n2 = np.maximum(n, 1) ** 2
    large = np.minimum(np.floor(np.log2(n2.astype(np.float64))).astype(np.int64) + 2, nb - 1)
    return (np.where(rel > 0, nb, 0) + np.where(n < max_exact, n, large)).astype(np.int32)


def _t5_vals(table, rel):
    idx = _t5_bucket_np(rel)
    v = jnp.take(table.astype(F32), jnp.asarray(idx.reshape(-1)), axis=0)
    return jnp.moveaxis(v, -1, 0).reshape((table.shape[1],) + idx.shape)


def _t5_dense(table, rel):
    nb = T5_BUCKETS // 2
    max_exact = nb // 2
    n = jnp.abs(rel)
    steps = [int(math.ceil(math.sqrt(2.0 ** k))) for k in range(7, 7 + nb - 1 - max_exact)]
    large = max_exact + sum((n >= t).astype(jnp.int32) for t in steps)
    bucket = jnp.where(rel > 0, nb, 0) + jnp.where(n < max_exact, n, large)
    table = table.astype(F32)
    out = jnp.zeros((table.shape[1],) + rel.shape, F32)
    for b in range(T5_BUCKETS):
        out = jnp.where((bucket == b)[None], table[b].reshape((-1,) + (1,) * rel.ndim), out)
    return out


def _toeplitz(w, n, m):
    p = n + m - 1
    wp = jnp.concatenate([w, jnp.zeros(w.shape[:-1] + (1,), w.dtype)], axis=-1)
    flat = jnp.tile(wp, (1,) * (w.ndim - 1) + (n,))[..., :n * p]
    return flat.reshape(w.shape[:-1] + (n, p))[..., n - 1:n - 1 + m]


def _pad_lanes_neg(x):
    pad = jnp.full(x.shape[:-1] + (LANES - x.shape[-1],), NEG, F32)
    return jnp.concatenate([x, pad], axis=-1)


def _t5_tables(t5_table):
    table = t5_table.astype(F32) * LOG2E
    ta = table[:, :A_HEADS]
    tc = table[:, A_HEADS:]
    out = {}
    rel = (lax.broadcasted_iota(jnp.int32, (BIAS_TALL, TQ), 0) - 3 * TQ
           - lax.broadcasted_iota(jnp.int32, (BIAS_TALL, TQ), 1))
    out["a_tall"] = _t5_dense(ta, rel)
    out["a_far"] = _t5_vals(ta, np.array([-(SEQ + N_META), SEQ + N_META])).T
    rel_t = np.arange(N_META)[:, None] - (N_META + np.arange(TQ))[None, :]
    far_t = np.full((1, 1), -(SEQ + N_META))

    def meta_key_bias_t(tab):
        rest = jnp.broadcast_to(_t5_vals(tab, far_t), (tab.shape[1], N_META, SEQ - TQ))
        return jnp.concatenate([_t5_vals(tab, rel_t), rest], axis=-1)

    out["a_biasm_t"] = meta_key_bias_t(ta)
    out["c_biasm_t"] = meta_key_bias_t(tc)
    offs = np.array([0, -WINDOW, -(GQA_KW - TQ)])
    shape = (3, GQA_KW, TQ)
    relb = (lax.broadcasted_iota(jnp.int32, shape, 1) - lax.broadcasted_iota(jnp.int32, shape, 2)
            + jnp.asarray(offs, jnp.int32)[:, None, None])
    band = jnp.where((jnp.abs(relb) <= WINDOW)[None], _t5_dense(tc, relb), NEG)
    out["c_band_t"] = jnp.swapaxes(band, 0, 1)
    mq = np.arange(N_META)[:, None]
    near_k = N_META + np.arange(TK)[None, :] - mq
    far_k = np.full((1, 1), SEQ + N_META)
    first = _t5_vals(ta, near_k)
    rest = jnp.broadcast_to(_t5_vals(ta, far_k), (A_HEADS, N_META, SEQ - TK))
    out["a_mq_real"] = jnp.concatenate([first, rest], axis=-1)
    out["a_mq_meta"] = _pad_lanes_neg(_t5_vals(ta, np.arange(N_META)[None, :] - mq))
    rel0 = N_META + np.arange(WINDOW)[None, :] - mq
    out["c_mq_real"] = jnp.where(jnp.asarray(rel0 <= WINDOW)[None], _t5_vals(tc, rel0), NEG)
    out["c_mq_meta"] = _pad_lanes_neg(_t5_vals(tc, np.arange(N_META)[None, :] - mq))
    return out


def _nat_bias_t(rpb):
    n_rows = SEQ // GRID_W
    w = jnp.pad(rpb.astype(F32) * LOG2E, ((0, 0), (0, 0), (GRID_W - NA_COLS, GRID_W - NA_COLS)))
    tiles = _toeplitz(w, GRID_W, GRID_W)
    cols = np.arange(GRID_W)
    cstart = np.clip(cols - NA_COLS // 2, 0, GRID_W - NA_COLS)
    col_ok = (cols[None, :] >= cstart[:, None]) & (cols[None, :] < cstart[:, None] + NA_COLS)
    tiles = jnp.where(jnp.asarray(col_ok)[None, None], tiles, NEG)
    neg_tile = jnp.full((B_HEADS, 1, GRID_W, GRID_W), NEG, F32)
    tiles = jnp.concatenate([tiles, neg_tile], axis=1)
    n_blk = n_rows // NAT_R
    idx = np.zeros((3, NAT_R, NAT_W), np.int32)
    for v, blk in enumerate((0, 1, n_blk - 1)):
        r0 = NAT_R * blk
        w0 = int(np.clip(r0 - NA_ROWS // 2, 0, n_rows - NAT_W))
        for qr in range(NAT_R):
            r = r0 + qr
            rs = int(np.clip(r - NA_ROWS // 2, 0, n_rows - NA_ROWS))
            for kr in range(NAT_W):
                krow = w0 + kr
                idx[v, qr, kr] = krow - r + NA_ROWS - 1 if rs <= krow < rs + NA_ROWS else 2 * NA_ROWS - 1
    g = jnp.take(tiles, jnp.asarray(idx.reshape(-1)), axis=1)
    g = g.reshape(B_HEADS, 3, NAT_R, NAT_W, GRID_W, GRID_W)
    g = jnp.transpose(g, (1, 0, 3, 5, 2, 4))
    return g.reshape(3, B_HEADS, NAT_W * GRID_W, NAT_R * GRID_W)


def _prep_w_in(w):
    col = np.arange(IN_COLS)
    is_q = (col < 512) | ((col >= 1536) & (col < 2048)) | ((col >= 3072) & (col < 3584))
    w = w * jnp.asarray(np.where(is_q, SCALE * LOG2E, 1.0), F32)[None, :]
    qc = w[:, 3072:3584].reshape(D_MODEL, C_HEADS, HEAD_DIM)[:, np.asarray(C_PERM)].reshape(D_MODEL, 512)
    w = jnp.concatenate([w[:, :3072], qc, w[:, 3584:]], axis=1)
    return w.astype(BF16)


def _meta_vt(p_m, lo, groups, nb):
    v = p_m[:, lo:lo + groups * LANES].reshape(nb, N_META, groups, LANES)
    v = jnp.transpose(v, (0, 2, 3, 1))
    ones = jnp.ones((nb, groups, VT_ROWS - LANES, N_META), BF16)
    return jnp.concatenate([v, ones], axis=2).reshape(nb, groups * VT_ROWS, N_META)


def kernel(x, meta_tokens, t5_table, norm_ffn1, w_ffn1_in, w_ffn1_out, norm_mix, w_in, lambda_q1, lambda_k1,
           lambda_q2, lambda_k2, subln_gain, natten_rpb, sink_logits, w_branch, w_gate, w_out, norm_ffn2,
           w_ffn2_in, w_ffn2_out, final_norm):
    nb, seq, d = x.shape
    assert (seq, d) == (SEQ, D_MODEL)
    depth = norm_ffn1.shape[0]
    h_r = x.reshape(nb * SEQ, D_MODEL)
    h_m = jnp.broadcast_to(meta_tokens[None].astype(x.dtype), (nb, N_META, D_MODEL)).reshape(nb * N_META, D_MODEL)
    t5 = _t5_tables(t5_table)
    perm = np.asarray(C_PERM)
    out = None
    for l in range(depth):
        lam_init = 0.8 - 0.6 * math.exp(-0.3 * l)
        w1i, w1o = w_ffn1_in[l].astype(BF16), w_ffn1_out[l].astype(BF16)
        w2i, w2o = w_ffn2_in[l].astype(BF16), w_ffn2_out[l].astype(BF16)
        wi = _prep_w_in(w_in[l])
        wvt = jnp.concatenate([wi[:, 1024:1536], wi[:, 2560:3072], wi[:, 3712:3840]], axis=1).T
        wg = w_gate[l].astype(BF16)
        wb2 = w_branch[l, 2].reshape(C_HEADS, HEAD_DIM, D_MODEL)[perm].reshape(BRANCH_W, D_MODEL)
        wb = jnp.stack([w_branch[l, 0], w_branch[l, 1], wb2]).astype(BF16)
        wo = w_out[l].astype(BF16)
        lvecs = [v[l].reshape(1, HEAD_DIM).astype(F32) for v in (lambda_q1, lambda_k1, lambda_q2, lambda_k2)]
        g_sub = subln_gain[l].reshape(1, LANES).astype(F32)
        sink = sink_logits[l].astype(F32)[perm] * LOG2E
        nat_bias = _nat_bias_t(natten_rpb[l])

        h_r = _ffn(h_r, norm_ffn1[l], w1i, w1o)
        h_m = _ffn(h_m, norm_ffn1[l], w1i, w1o)

        p_r, vta_r, vtb_r, vtc_r = _inproj(h_r, norm_mix[l], wi, wvt)
        p_m = _inproj(h_m, norm_mix[l], wi)
        vta_m = _meta_vt(p_m, 1024, VT_GROUPS[0], nb)
        vtb_m = _meta_vt(p_m, 2560, VT_GROUPS[1], nb)
        vtc_m = _meta_vt(p_m, 3712, VT_GROUPS[2], nb)
        ya_r = _diff_real(p_r, vta_r, p_m, vta_m, lvecs, g_sub.reshape(LANES, 1), t5["a_far"], t5["a_tall"],
                          t5["a_biasm_t"], lam_init, nb)
        yb_r = _nat_real(p_r, vtb_r, p_m, vtb_m, nat_bias, nb)
        yc_r = _gqa_real(p_r, vtc_r, p_m, vtc_m, sink, t5["c_band_t"], t5["c_biasm_t"], nb)
        ya_m, yb_m, yc_m = _meta_queries(p_r, p_m, sink, lvecs, g_sub, t5["a_mq_real"], t5["a_mq_meta"],
                                         t5["c_mq_real"], t5["c_mq_meta"], lam_init, nb)
        h_r = _merge(h_r, norm_mix[l], ya_r, yb_r, yc_r, wg, wb, wo)
        h_m = _merge(h_m, norm_mix[l], ya_m, yb_m, yc_m, wg, wb, wo)

        last = l == depth - 1
        h_r = _ffn(h_r, norm_ffn2[l], w2i, w2o, final_gain=final_norm if last else None)
        h_m = _ffn(h_m, norm_ffn2[l], w2i, w2o)
        out = h_r
    return out.reshape(nb, SEQ, D_MODEL)
```
